```python
import jax, jax.numpy as jnp
from jax import lax
import numpy as np

D_MODEL = 1024
BATCH = 8
SEQ = 8192
DEPTH = 4

N_MIXERS = 2
RMS_EPS = 1e-6
L2_EPS = 1e-6

GDN_HEADS = 8
GDN_DK = 128
GDN_DV = 128
GDN_CONV = 4
GDN_CHUNK = 64
GDN_QK_WIDTH = GDN_HEADS * GDN_DK
GDN_V_WIDTH = GDN_HEADS * GDN_DV
GDN_IN_WIDTH = 2 * GDN_QK_WIDTH + 2 * GDN_V_WIDTH + 2 * GDN_HEADS

DIL_GROUPS = ((128, 1), (512, 4), (2048, 16))
N_DIL_GROUPS = 3
DIL_HEADS_PER_GROUP = 8
DIL_HEAD_DIM = 64
DIL_TOTAL_HEADS = N_DIL_GROUPS * DIL_HEADS_PER_GROUP
DIL_IN_WIDTH = 3 * DIL_TOTAL_HEADS * DIL_HEAD_DIM
DIL_OUT_WIDTH = DIL_HEADS_PER_GROUP * DIL_HEAD_DIM
ALIBI_MAX_BIAS = 8.0

FFN_HIDDEN = -(-8 * D_MODEL // (3 * 256)) * 256

kernel_name = "hybrid_gdn_dilated_swa_swiglu"


def _rmsnorm(x, w):
    xf = x.astype(jnp.float32)
    y = xf * lax.rsqrt(jnp.mean(xf * xf, axis=-1, keepdims=True) + RMS_EPS)
    return (y * w.astype(jnp.float32)).astype(x.dtype)


def _l2norm(x):
    xf = x.astype(jnp.float32)
    return xf * lax.rsqrt(jnp.sum(xf * xf, axis=-1, keepdims=True) + L2_EPS)


def _causal_depthwise_conv(x, w):
    c = x.shape[-1]
    return lax.conv_general_dilated(
        x, w[:, None, :].astype(x.dtype), window_strides=(1,),
        padding=[(w.shape[0] - 1, 0)], dimension_numbers=("NWC", "WIO", "NWC"),
        feature_group_count=c)


def _chunk_gated_delta_rule(q, k, v, g, beta):
    b, s, h, dk = q.shape
    dv = v.shape[-1]
    c = GDN_CHUNK
    nc = s // c

    def chunks(t):
        return t.reshape(b, nc, c, h, -1).transpose(0, 3, 1, 2, 4)

    q, k, v = chunks(q), chunks(k), chunks(v)
    g = g.reshape(b, nc, c, h).transpose(0, 3, 1, 2)
    beta = beta.reshape(b, nc, c, h).transpose(0, 3, 1, 2)
    gc = jnp.cumsum(g, axis=-1)
    idx = jnp.arange(c)
    causal = idx[:, None] >= idx[None, :]
    strict = idx[:, None] > idx[None, :]
    decay = jnp.exp(jnp.where(causal, gc[..., :, None] - gc[..., None, :], -jnp.inf))
    kk = jnp.einsum("bhncd,bhnmd->bhncm", k, k)
    a_mat = jnp.where(strict, kk * beta[..., :, None] * decay, 0.0) + jnp.eye(c, dtype=jnp.float32)
    rhs = jnp.concatenate([v * beta[..., None], k * (beta * jnp.exp(gc))[..., None]], axis=-1)
    sol = lax.linalg.triangular_solve(a_mat, rhs, left_side=True, lower=True, unit_diagonal=True)
    u, w = sol[..., :dv], sol[..., dv:]
    qk = jnp.einsum("bhncd,bhnmd->bhncm", q, k) * decay
    q_dec = q * jnp.exp(gc)[..., None]
    k_tail = k * jnp.exp(gc[..., -1:] - gc)[..., None]
    c_dec = jnp.exp(gc[..., -1])
    xs = (jnp.moveaxis(qk, 2, 0), jnp.moveaxis(q_dec, 2, 0), jnp.moveaxis(k_tail, 2, 0),
          jnp.moveaxis(u, 2, 0), jnp.moveaxis(w, 2, 0), jnp.moveaxis(c_dec, 2, 0))

    def step(state, inp):
        qk_c, qd_c, kt_c, u_c, w_c, cd_c = inp
        v_new = u_c - jnp.einsum("bhck,bhkv->bhcv", w_c, state)
        o_c = (jnp.einsum("bhck,bhkv->bhcv", qd_c, state)
               + jnp.einsum("bhcm,bhmv->bhcv", qk_c, v_new))
        state = state * cd_c[..., None, None] + jnp.einsum("bhck,bhcv->bhkv", kt_c, v_new)
        return state, o_c

    s0 = jnp.zeros((b, h, dk, dv), jnp.float32)
    _, o = lax.scan(step, s0, xs)
    return o.transpose(1, 0, 3, 2, 4).reshape(b, s, h, dv)


def _gated_deltanet(h, w_in, conv_w, a_log, dt_bias, norm_w, w_out):
    b, s, _ = h.shape
    proj = h @ w_in
    n_qkv = 2 * GDN_QK_WIDTH + GDN_V_WIDTH
    qkv = jax.nn.silu(_causal_depthwise_conv(proj[..., :n_qkv], conv_w))
    z = proj[..., n_qkv:n_qkv + GDN_V_WIDTH].reshape(b, s, GDN_HEADS, GDN_DV)
    a = proj[..., n_qkv + GDN_V_WIDTH:n_qkv + GDN_V_WIDTH + GDN_HEADS].astype(jnp.float32)
    bb = proj[..., n_qkv + GDN_V_WIDTH + GDN_HEADS:].astype(jnp.float32)
    q = _l2norm(qkv[..., :GDN_QK_WIDTH].reshape(b, s, GDN_HEADS, GDN_DK)) * (GDN_DK ** -0.5)
    k = _l2norm(qkv[..., GDN_QK_WIDTH:2 * GDN_QK_WIDTH].reshape(b, s, GDN_HEADS, GDN_DK))
    v = qkv[..., 2 * GDN_QK_WIDTH:].reshape(b, s, GDN_HEADS, GDN_DV).astype(jnp.float32)
    beta = jax.nn.sigmoid(bb)
    g = -jnp.exp(a_log.astype(jnp.float32)) * jax.nn.softplus(a + dt_bias.astype(jnp.float32))
    o = _chunk_gated_delta_rule(q, k, v, g, beta)
    o = o * lax.rsqrt(jnp.mean(o * o, axis=-1, keepdims=True) + RMS_EPS)
    o = o * norm_w.astype(jnp.float32) * jax.nn.silu(z.astype(jnp.float32))
    return o.astype(h.dtype).reshape(b, s, GDN_V_WIDTH) @ w_out


def _dilated_group_attention(q, k, v, slopes, dilation, span):
    b, s, h, dh = q.shape
    n_sub = -(-s // dilation)
    nb = -(-n_sub // span)
    l_pad = nb * span
    t_pad = l_pad * dilation

    def to_blocks(t):
        t = jnp.pad(t, ((0, 0), (0, t_pad - s), (0, 0), (0, 0)))
        t = t.reshape(b, l_pad, dilation, h, dh).transpose(0, 2, 1, 3, 4)
        return t.reshape(b, dilation, nb, span, h, dh)

    def with_prev(t):
        prev = jnp.pad(t, ((0, 0), (0, 0), (1, 0), (0, 0), (0, 0), (0, 0)))[:, :, :-1]
        return jnp.concatenate([prev, t], axis=3)

    def from_blocks(t):
        e = t.shape[-1]
        t = t.reshape(b, dilation, l_pad, h, e).transpose(0, 2, 1, 3, 4)
        return t.reshape(b, t_pad, h, e)[:, :s]

    qb = to_blocks(q)
    kb = with_prev(to_blocks(k))
    vb = with_prev(to_blocks(v))
    scores = jnp.einsum("brnqhd,brnkhd->brnhqk", qb, kb,
                        preferred_element_type=jnp.float32) * (dh ** -0.5)
    qi = jnp.arange(span)
    kj = jnp.arange(2 * span)
    steps = qi[:, None] + span - kj[None, :]
    key_sub = jnp.arange(nb)[:, None] * span - span + kj[None, :]
    valid = ((steps >= 0) & (steps <= span))[None] & (key_sub >= 0)[:, None, :]
    bias = -slopes.astype(jnp.float32)[:, None, None] * (steps * dilation).astype(jnp.float32)
    logits = jnp.where(valid[:, None], scores + bias, -jnp.inf)
    m = jnp.max(logits, axis=-1, keepdims=True)
    p = jnp.exp(logits - m)
    l = jnp.sum(p, axis=-1)
    o = jnp.einsum("brnhqk,brnkhd->brnqhd", p, vb.astype(jnp.float32))
    o = o / l.transpose(0, 1, 2, 4, 3)[..., None]
    m_out = from_blocks(m[..., 0].transpose(0, 1, 2, 4, 3)[..., None])[..., 0]
    l_out = from_blocks(l.transpose(0, 1, 2, 4, 3)[..., None])[..., 0]
    return from_blocks(o), m_out, l_out


def _dilated_attention(h, w_in, q_norm, k_norm, w_out):
    b, s, _ = h.shape
    qkv = (h @ w_in).reshape(b, s, 3, N_DIL_GROUPS, DIL_HEADS_PER_GROUP, DIL_HEAD_DIM)
    q = _rmsnorm(qkv[:, :, 0], q_norm)
    k = _rmsnorm(qkv[:, :, 1], k_norm)
    v = qkv[:, :, 2]
    slopes = (2.0 ** (-ALIBI_MAX_BIAS * jnp.arange(1, DIL_TOTAL_HEADS + 1, dtype=jnp.float32)
                      / DIL_TOTAL_HEADS)).reshape(N_DIL_GROUPS, DIL_HEADS_PER_GROUP)
    outs, maxes, dens = [], [], []
    for gi, (window, dilation) in enumerate(DIL_GROUPS):
        o_g, m_g, l_g = _dilated_group_attention(q[:, :, gi], k[:, :, gi], v[:, :, gi],
                                                 slopes[gi], dilation, window // dilation)
        outs.append(o_g)
        maxes.append(m_g)
        dens.append(l_g)
    o_all = jnp.stack(outs)
    m_all = jnp.stack(maxes)
    l_all = jnp.stack(dens)
    wts = l_all * jnp.exp(m_all - jnp.max(m_all, axis=0, keepdims=True))
    o = jnp.sum(wts[..., None] * o_all, axis=0) / jnp.sum(wts, axis=0)[..., None]
    return o.astype(h.dtype).reshape(b, s, DIL_OUT_WIDTH) @ w_out


def _swiglu(h, w_in, w_out):
    gu = h @ w_in
    return (jax.nn.silu(gu[..., :FFN_HIDDEN]) * gu[..., FFN_HIDDEN:]) @ w_out


def setup_inputs(seed: int = 0) -> dict:
    key = jax.random.key(seed)
    ks = jax.random.split(key, 16)
    f32 = jnp.float32
    n_a = (DEPTH + 1) // 2
    n_b = DEPTH // 2
    out_scale = (2.0 * DEPTH) ** -0.5

    def nrm(k, shape, scale):
        return jax.random.normal(k, shape, f32) * scale

    def gain(k, shape):
        return 1.0 + 0.02 * jax.random.normal(k, shape, f32)

    dt = jnp.exp(jax.random.uniform(ks[6], (n_a, GDN_HEADS), f32, np.log(1e-3), np.log(1e-1)))
    return {
        "x": nrm(ks[0], (BATCH, SEQ, D_MODEL), 1.0),
        "norm_mix": gain(ks[1], (DEPTH, D_MODEL)),
        "norm_ffn": gain(ks[2], (DEPTH, D_MODEL)),
        "gdn_w_in": nrm(ks[3], (n_a, D_MODEL, GDN_IN_WIDTH), D_MODEL ** -0.5),
        "gdn_conv_w": nrm(ks[4], (n_a, GDN_CONV, 2 * GDN_QK_WIDTH + GDN_V_WIDTH), GDN_CONV ** -0.5),
        "gdn_a_log": jnp.log(jax.random.uniform(ks[5], (n_a, GDN_HEADS), f32, 1.0, 16.0)),
        "gdn_dt_bias": dt + jnp.log(-jnp.expm1(-dt)),
        "gdn_norm_w": gain(ks[7], (n_a, GDN_DV)),
        "gdn_w_out": nrm(ks[8], (n_a, GDN_V_WIDTH, D_MODEL), GDN_V_WIDTH ** -0.5 * out_scale),
        "dil_w_in": nrm(ks[9], (n_b, D_MODEL, DIL_IN_WIDTH), D_MODEL ** -0.5),
        "dil_q_norm": gain(ks[10], (n_b, DIL_HEAD_DIM)),
        "dil_k_norm": gain(ks[11], (n_b, DIL_HEAD_DIM)),
        "dil_w_out": nrm(ks[12], (n_b, DIL_OUT_WIDTH, D_MODEL), DIL_OUT_WIDTH ** -0.5 * out_scale),
        "ffn_w_in": nrm(ks[13], (DEPTH, D_MODEL, 2 * FFN_HIDDEN), D_MODEL ** -0.5),
        "ffn_w_out": nrm(ks[14], (DEPTH, FFN_HIDDEN, D_MODEL), FFN_HIDDEN ** -0.5 * out_scale),
    }


def reference(x, norm_mix, norm_ffn, gdn_w_in, gdn_conv_w, gdn_a_log, gdn_dt_bias, gdn_norm_w,
              gdn_w_out, dil_w_in, dil_q_norm, dil_k_norm, dil_w_out, ffn_w_in, ffn_w_out):
    for i in range(DEPTH):
        j = i // N_MIXERS
        hn = _rmsnorm(x, norm_mix[i])
        if i % N_MIXERS == 0:
            x = x + _gated_deltanet(hn, gdn_w_in[j], gdn_conv_w[j], gdn_a_log[j], gdn_dt_bias[j],
                                    gdn_norm_w[j], gdn_w_out[j])
        else:
            x = x + _dilated_attention(hn, dil_w_in[j], dil_q_norm[j], dil_k_norm[j], dil_w_out[j])
        x = x + _swiglu(_rmsnorm(x, norm_ffn[i]), ffn_w_in[i], ffn_w_out[i])
    return x
```

```python
import functools

import jax
import jax.numpy as jnp
from jax import lax
from jax.experimental import pallas as pl
from jax.experimental.pallas import tpu as pltpu

F32 = jnp.float32
BF16 = jnp.bfloat16

RMS_EPS = 1e-6
L2_EPS = 1e-6

N_HEADS = 8
GDN_DK = 128
GDN_CONV = 4
CHUNK = 64
INV_BASE = 8

ATT_GROUPS = ((128, 1), (512, 4), (2048, 16))
ATT_SPAN = 128
ATT_DH = 64
ATT_WIDTH = N_HEADS * ATT_DH
ALIBI_MAX_BIAS = 8.0

LANES = 128
VMEM_LIMIT = 56 * 1024 * 1024


def _cparams(sem):
    return pltpu.CompilerParams(dimension_semantics=sem, vmem_limit_bytes=VMEM_LIMIT)


def _resident(shape):
    nd = len(shape)
    return pl.BlockSpec(shape, lambda *_: (0,) * nd, pipeline_mode=pl.Buffered(1))


def _rms_scale(x, gain):
    ms = jnp.mean(x * x, axis=-1, keepdims=True)
    return x * lax.rsqrt(ms + RMS_EPS) * gain


def _sigmoid(x):
    return 1.0 / (1.0 + jnp.exp(-x))


def _dot(a, b):
    return jnp.dot(a, b, preferred_element_type=F32)


def _dot_nt(a, b):
    return lax.dot_general(a, b, (((1,), (1,)), ((), ())), preferred_element_type=F32)


def _dot_tn(a, b):
    return lax.dot_general(a, b, (((0,), (0,)), ((), ())), preferred_element_type=F32)


def _norm_proj_kernel(x_ref, gain_ref, w_ref, o_ref, *, tn):
    xn = _rms_scale(x_ref[...], gain_ref[...]).astype(BF16)
    for j in range(w_ref.shape[1] // tn):
        o_ref[:, j * tn:(j + 1) * tn] = _dot(xn, w_ref[:, j * tn:(j + 1) * tn]).astype(o_ref.dtype)


def _norm_proj(x, gain, w, *, tm=512, tn=512):
    n, d = x.shape
    wout = w.shape[1]
    return pl.pallas_call(
        functools.partial(_norm_proj_kernel, tn=tn),
        grid=(n // tm,),
        in_specs=[pl.BlockSpec((tm, d), lambda i: (i, 0)),
                  _resident((1, d)),
                  _resident((d, wout))],
        out_specs=pl.BlockSpec((tm, wout), lambda i: (i, 0)),
        out_shape=jax.ShapeDtypeStruct((n, wout), BF16),
        compiler_params=_cparams(("parallel",)),
        name="norm_proj",
    )(x, gain, w)


def _gdn_proj_kernel(x_ref, gain_ref, w_ref, wab_ref, gconst_ref, proj_ref, colpack_ref, gct_ref,
                     *, tn):
    tm = x_ref.shape[0]
    xn = _rms_scale(x_ref[...], gain_ref[...]).astype(BF16)
    for j in range(w_ref.shape[1] // tn):
        proj_ref[:, j * tn:(j + 1) * tn] = _dot(xn, w_ref[:, j * tn:(j + 1) * tn]).astype(BF16)

    ab = _dot(xn, wab_ref[...])
    lane = lax.broadcasted_iota(jnp.int32, (tm, LANES), 1)
    row = lax.broadcasted_iota(jnp.int32, (tm, LANES), 0)
    head_lanes = lane < N_HEADS
    neg_a = gconst_ref[0:1, :]
    dt_bias = gconst_ref[1:2, :]
    sp_in = ab + dt_bias
    softplus = jnp.maximum(sp_in, 0.0) + jnp.log(1.0 + jnp.exp(-jnp.abs(sp_in)))
    g = neg_a * softplus
    beta = _sigmoid(ab)

    rc = row & (CHUNK - 1)
    gc = g
    s = 1
    while s < CHUNK:
        gc = gc + jnp.where(rc >= s, pltpu.roll(gc, s, axis=0), 0.0)
        s *= 2
    gc3 = gc.reshape(tm // CHUNK, CHUNK, LANES)
    gl = jnp.broadcast_to(gc3[:, CHUNK - 1:CHUNK, :], gc3.shape).reshape(tm, LANES)

    eg = jnp.exp(gc)
    beta0 = pltpu.roll(beta, LANES - N_HEADS, axis=1)

    def put(v, at):
        v = jnp.where(head_lanes, v, 0.0)
        return pltpu.roll(v, at, axis=1) if at else v

    colpack_ref[...] = (put(gc, 0) + jnp.where((lane >= 8) & (lane < 16), beta, 0.0)
                        + put(eg, 16) + put(jnp.exp(gl - gc), 24) + put(jnp.exp(gl), 32)
                        + put(beta0 * eg, 40))

    sel = (lax.broadcasted_iota(jnp.int32, (N_HEADS, LANES), 0)
           == lax.broadcasted_iota(jnp.int32, (N_HEADS, LANES), 1)).astype(BF16)
    gcm = jnp.where(head_lanes, gc, 0.0)
    hi = gcm.astype(BF16)
    r1 = gcm - hi.astype(F32)
    mid = r1.astype(BF16)
    lo = (r1 - mid.astype(F32)).astype(BF16)
    for c in range(tm // CHUNK):
        sl = slice(c * CHUNK, (c + 1) * CHUNK)
        gct_ref[c] = (_dot_nt(sel, hi[sl]) + _dot_nt(sel, mid[sl])) + _dot_nt(sel, lo[sl])


def _gdn_proj(x, gain, w, wab, gconst, *, tm=512, tn=512):
    n, d = x.shape
    wout = w.shape[1]
    return pl.pallas_call(
        functools.partial(_gdn_proj_kernel, tn=tn),
        grid=(n // tm,),
        in_specs=[pl.BlockSpec((tm, d), lambda i: (i, 0)),
                  _resident((1, d)),
                  _resident((d, wout)),
                  _resident((d, LANES)),
                  _resident((8, LANES))],
        out_specs=[pl.BlockSpec((tm, wout), lambda i: (i, 0)),
                   pl.BlockSpec((tm, LANES), lambda i: (i, 0)),
                   pl.BlockSpec((tm // CHUNK, N_HEADS, CHUNK), lambda i: (i, 0, 0))],
        out_shape=[jax.ShapeDtypeStruct((n, wout), BF16),
                   jax.ShapeDtypeStruct((n, LANES), F32),
                   jax.ShapeDtypeStruct((n // CHUNK, N_HEADS, CHUNK), F32)],
        compiler_params=_cparams(("parallel",)),
        name="gdn_proj",
    )(x, gain, w, wab, gconst)


def _unit_lower_inverse(nm, masks):
    eye, m_base, level_masks = masks
    nb = jnp.where(m_base, nm, 0.0)
    x = eye - nb
    nb16 = nb.astype(BF16)
    p = _dot(nb16, nb16)
    s = 2
    while s < INV_BASE:
        p16 = p.astype(BF16)
        s *= 2
        if s < INV_BASE:
            both = _dot(p16, jnp.concatenate([p16, x.astype(BF16)], axis=1))
            p = both[:, :CHUNK]
            x = x + both[:, CHUNK:]
        else:
            x = x + _dot(p16, x.astype(BF16))
    for m_off in level_masks:
        x16 = x.astype(BF16)
        w = _dot(jnp.where(m_off, nm, 0.0).astype(BF16), x16)
        x = x - _dot(x16, w.astype(BF16))
    return x


def _gdn_chunk_kernel(q_ref, k_ref, v_ref, z_ref, colpack_ref, gct_ref, convw_ref, normw_ref,
                      o_ref, state_ref, halo_ref, qs_ref, ks_ref, vs_ref):
    t = pl.program_id(1)
    tt = q_ref.shape[0]
    width = N_HEADS * GDN_DK

    @pl.when(t == 0)
    def _():
        state_ref[...] = jnp.zeros_like(state_ref)
        halo_ref[...] = jnp.zeros_like(halo_ref)

    for idx, (src, dst) in enumerate(((q_ref, qs_ref), (k_ref, ks_ref), (v_ref, vs_ref))):
        for h in range(N_HEADS):
            cols = slice(h * GDN_DK, (h + 1) * GDN_DK)
            hcols = slice(idx * width + h * GDN_DK, idx * width + (h + 1) * GDN_DK)
            x = src[:, cols].astype(F32)
            xx = jnp.concatenate([halo_ref[:, hcols], x], axis=0)
            halo_ref[:, hcols] = x[tt - 8:, :]
            y = convw_ref[GDN_CONV - 1:GDN_CONV, hcols] * x
            for j in range(GDN_CONV - 1):
                off = 8 - (GDN_CONV - 1) + j
                y = y + convw_ref[j:j + 1, hcols] * xx[off:off + tt, :]
            y = y * _sigmoid(y)
            if idx < 2:
                y = y * lax.rsqrt(jnp.sum(y * y, axis=-1, keepdims=True) + L2_EPS)
            if idx == 0:
                y = y * (GDN_DK ** -0.5)
            dst[:, cols] = y

    ri = lax.broadcasted_iota(jnp.int32, (CHUNK, CHUNK), 0)
    ci = lax.broadcasted_iota(jnp.int32, (CHUNK, CHUNK), 1)
    causal = ri >= ci
    strict = ri > ci
    eye = (ri == ci).astype(F32)

    def same_block(sz):
        shift = sz.bit_length() - 1
        return (ri >> shift) == (ci >> shift)

    level_masks = []
    sz = INV_BASE
    while sz < CHUNK:
        level_masks.append(same_block(2 * sz) & jnp.logical_not(same_block(sz)))
        sz *= 2
    masks = (eye, same_block(INV_BASE), tuple(level_masks))
    normw = normw_ref[...]

    def chunk_body(c, carry):
        r0 = pl.multiple_of(c * CHUNK, CHUNK)
        rows = pl.ds(r0, CHUNK)
        cp = colpack_ref[rows, :]
        gt = gct_ref[c]
        for h in range(N_HEADS):
            cols = slice(h * GDN_DK, (h + 1) * GDN_DK)
            q = qs_ref[rows, cols]
            k = ks_ref[rows, cols]
            v = vs_ref[rows, cols]
            gc = cp[:, h:h + 1]
            beta = cp[:, 8 + h:9 + h]
            eg = cp[:, 16 + h:17 + h]
            etail = cp[:, 24 + h:25 + h]
            elast = cp[0:1, 32 + h:33 + h]
            beg = cp[:, 40 + h:41 + h]
            grow = gt[h:h + 1, :]
            decay = jnp.exp(jnp.where(causal, gc - grow, -jnp.inf))

            k16 = k.astype(BF16)
            qk_kk = _dot_nt(jnp.concatenate([q.astype(BF16), k16], axis=0), k16)
            qk = qk_kk[:CHUNK] * decay
            nm = jnp.where(strict, qk_kk[CHUNK:] * beta * decay, 0.0)
            tinv = _unit_lower_inverse(nm, masks)
            rhs = jnp.concatenate([v * beta, k * beg], axis=1).astype(BF16)
            sol = _dot(tinv.astype(BF16), rhs)
            u = sol[:, :GDN_DK]
            w = sol[:, GDN_DK:]

            state = state_ref[h]
            s16 = state.astype(BF16)
            ws_qs = _dot(jnp.concatenate([w.astype(BF16), (q * eg).astype(BF16)], axis=0), s16)
            v_new = u - ws_qs[:CHUNK]
            vn16 = v_new.astype(BF16)
            o = ws_qs[CHUNK:] + _dot(qk.astype(BF16), vn16)
            state_ref[h] = state * elast + _dot_tn((k * etail).astype(BF16), vn16)

            o = o * lax.rsqrt(jnp.mean(o * o, axis=-1, keepdims=True) + RMS_EPS)
            z = z_ref[rows, cols].astype(F32)
            o_ref[rows, cols] = (o * normw * (z * _sigmoid(z))).astype(o_ref.dtype)
        return carry

    lax.fori_loop(0, tt // CHUNK, chunk_body, 0)


def _gdn_chunk(proj, colpack, gct, conv_w, norm_w, *, batch, seq, tt=256):
    n = batch * seq
    nt = seq // tt
    width = N_HEADS * GDN_DK

    def col_block(cb):
        return pl.BlockSpec((tt, width), lambda b, t: (b * nt + t, cb))

    return pl.pallas_call(
        _gdn_chunk_kernel,
        grid=(batch, nt),
        in_specs=[col_block(0), col_block(1), col_block(2), col_block(3),
                  pl.BlockSpec((tt, LANES), lambda b, t: (b * nt + t, 0)),
                  pl.BlockSpec((tt // CHUNK, N_HEADS, CHUNK), lambda b, t: (b * nt + t, 0, 0)),
                  _resident((GDN_CONV, 3 * width)),
                  _resident((1, GDN_DK))],
        out_specs=pl.BlockSpec((tt, width), lambda b, t: (b * nt + t, 0)),
        out_shape=jax.ShapeDtypeStruct((n, width), BF16),
        scratch_shapes=[pltpu.VMEM((N_HEADS, GDN_DK, GDN_DK), F32),
                        pltpu.VMEM((8, 3 * width), F32),
                        pltpu.VMEM((tt, width), F32),
                        pltpu.VMEM((tt, width), F32),
                        pltpu.VMEM((tt, width), F32)],
        compiler_params=_cparams(("parallel", "arbitrary")),
        name="gdn_chunk",
    )(proj, proj, proj, proj, colpack, gct, conv_w, norm_w)


def _att_group_kernel(*refs, dilation, slopes, nq, has_prev, emit_lse):
    it = iter(refs)
    q_ref, k_ref, v_ref, qw_ref, kw_ref = (next(it) for _ in range(5))
    oprev_ref = next(it) if has_prev else None
    lprev_ref = next(it) if has_prev else None
    o_ref = next(it)
    lse_ref = next(it) if emit_lse else None
    kn_ref, vn_ref = next(it), next(it)

    n = pl.program_id(2)
    span = ATT_SPAN
    rows_per_step = nq * span

    @pl.when(n == 0)
    def _():
        kn_ref[0:span, :] = jnp.zeros((span, ATT_WIDTH), BF16)
        vn_ref[0:span, :] = jnp.zeros((span, ATT_WIDTH), BF16)

    lane = lax.broadcasted_iota(jnp.int32, (rows_per_step, LANES), 1)
    left = lane < ATT_DH

    def head_norm(x, w):
        sq = x * x
        s_left = jnp.sum(jnp.where(left, sq, 0.0), axis=-1, keepdims=True)
        s_all = jnp.sum(sq, axis=-1, keepdims=True)
        r_left = lax.rsqrt(s_left * (1.0 / ATT_DH) + RMS_EPS)
        r_right = lax.rsqrt((s_all - s_left) * (1.0 / ATT_DH) + RMS_EPS)
        return x * jnp.where(left, r_left, r_right) * w

    qn = []
    for p in range(N_HEADS // 2):
        cols = slice(p * LANES, (p + 1) * LANES)
        kn_ref[span:, cols] = head_norm(k_ref[:, cols].astype(F32), kw_ref[:, cols]).astype(BF16)
        qn.append(head_norm(q_ref[:, cols].astype(F32), qw_ref[:, cols] * (ATT_DH ** -0.5))
                  .astype(BF16))
    vn_ref[span:, :] = v_ref[...]

    qi = lax.broadcasted_iota(jnp.int32, (span, 2 * span), 0)
    kj = lax.broadcasted_iota(jnp.int32, (span, 2 * span), 1)
    steps = qi + span - kj
    band = (steps >= 0) & (steps <= span)
    dist = (steps * dilation).astype(F32)
    lane_s = lax.broadcasted_iota(jnp.int32, (span, LANES), 1)
    left_s = lane_s < ATT_DH

    first_key = jnp.where(n == 0, span, 0)
    for i in range(nq):
        valid = band & (kj >= first_key) if i == 0 else band
        qrows = slice(i * span, (i + 1) * span)
        krows = slice(i * span, (i + 2) * span)
        lse_tile = jnp.zeros((span, LANES), F32)
        for p in range(N_HEADS // 2):
            cols = slice(p * LANES, (p + 1) * LANES)
            kcat = kn_ref[krows, cols]
            vcat = vn_ref[krows, cols]
            q_pair = qn[p][qrows]
            halves = []
            for e in range(2):
                h = 2 * p + e
                qm = jnp.where(left_s if e == 0 else jnp.logical_not(left_s), q_pair,
                               jnp.zeros_like(q_pair))
                s = _dot_nt(qm, kcat) - slopes[h] * dist
                s = jnp.where(valid, s, -jnp.inf)
                m = jnp.max(s, axis=-1, keepdims=True)
                pr = jnp.exp(s - m)
                l = jnp.sum(pr, axis=-1, keepdims=True)
                o_h = _dot(pr.astype(BF16), vcat) / l
                lse_h = m + jnp.log(l)
                if has_prev:
                    lp = lprev_ref[qrows, h:h + 1]
                    mx = jnp.maximum(lp, lse_h)
                    w0 = jnp.exp(lp - mx)
                    w1 = jnp.exp(lse_h - mx)
                    o_h = (w0 * oprev_ref[qrows, cols] + w1 * o_h) / (w0 + w1)
                    lse_h = mx + jnp.log(w0 + w1)
                halves.append(o_h)
                if emit_lse:
                    lse_tile = jnp.where(lane_s == h, lse_h, lse_tile)
            o_ref[qrows, cols] = jnp.where(left_s, halves[0], halves[1]).astype(o_ref.dtype)
        if emit_lse:
            lse_ref[qrows, :] = lse_tile

    kn_ref[0:span, :] = kn_ref[rows_per_step:, :]
    vn_ref[0:span, :] = vn_ref[rows_per_step:, :]


def _att_group(proj, qw, kw, prev, *, gi, batch, seq, last):
    window, d = ATT_GROUPS[gi]
    assert window // d == ATT_SPAN
    n_groups = len(ATT_GROUPS)
    sub = seq // d
    nblk = sub // ATT_SPAN
    nq = min(4, nblk)
    rows = nq * ATT_SPAN
    ncol = 3 * n_groups
    slopes = tuple(2.0 ** (-ALIBI_MAX_BIAS * (gi * N_HEADS + h + 1) / (n_groups * N_HEADS))
                   for h in range(N_HEADS))
    proj_v = proj.reshape(batch, sub, d * ncol * ATT_WIDTH)

    def proj_block(which):
        return pl.BlockSpec((None, rows, ATT_WIDTH),
                            lambda b, r, n: (b, n, r * ncol + which * n_groups + gi))

    o_spec = pl.BlockSpec((None, rows, ATT_WIDTH), lambda b, r, n: (b, n, r))
    l_spec = pl.BlockSpec((None, rows, LANES), lambda b, r, n: (b, n, r))
    in_specs = [proj_block(0), proj_block(1), proj_block(2),
                _resident((1, ATT_WIDTH)), _resident((1, ATT_WIDTH))]
    args = [proj_v, proj_v, proj_v, qw, kw]
    if prev is not None:
        in_specs += [o_spec, l_spec]
        args += [prev[0].reshape(batch, sub, d * ATT_WIDTH), prev[1].reshape(batch, sub, d * LANES)]
    out_specs = [o_spec]
    out_shape = [jax.ShapeDtypeStruct((batch, sub, d * ATT_WIDTH), BF16 if last else F32)]
    if not last:
        out_specs.append(l_spec)
        out_shape.append(jax.ShapeDtypeStruct((batch, sub, d * LANES), F32))
    outs = pl.pallas_call(
        functools.partial(_att_group_kernel, dilation=d, slopes=slopes, nq=nq,
                          has_prev=prev is not None, emit_lse=not last),
        grid=(batch, d, nblk // nq),
        in_specs=in_specs,
        out_specs=out_specs,
        out_shape=out_shape,
        scratch_shapes=[pltpu.VMEM((ATT_SPAN + rows, ATT_WIDTH), BF16),
                        pltpu.VMEM((ATT_SPAN + rows, ATT_WIDTH), BF16)],
        compiler_params=_cparams(("parallel", "parallel", "arbitrary")),
        name=f"att_group{gi}",
    )(*args)
    o = outs[0].reshape(batch * seq, ATT_WIDTH)
    if last:
        return o
    return o, outs[1].reshape(batch * seq, LANES)


def _out_ffn_kernel(x_ref, o_ref, wmix_ref, gain_ref, win_ref, wout_ref, y_ref, *, th):
    hidden = wout_ref.shape[0]
    x1 = x_ref[...] + _dot(o_ref[...], wmix_ref[...])
    xn = _rms_scale(x1, gain_ref[...]).astype(BF16)
    acc = x1
    for j in range(hidden // th):
        gate = _dot(xn, win_ref[:, j * th:(j + 1) * th])
        up = _dot(xn, win_ref[:, hidden + j * th:hidden + (j + 1) * th])
        act = (gate * _sigmoid(gate) * up).astype(BF16)
        acc = acc + _dot(act, wout_ref[j * th:(j + 1) * th, :])
    y_ref[...] = acc


def _out_ffn(x, o, wmix, gain, win, wout, *, tm=512, th=1408):
    n, d = x.shape
    ko = o.shape[1]
    hidden = wout.shape[0]
    return pl.pallas_call(
        functools.partial(_out_ffn_kernel, th=th),
        grid=(n // tm,),
        in_specs=[pl.BlockSpec((tm, d), lambda i: (i, 0)),
                  pl.BlockSpec((tm, ko), lambda i: (i, 0)),
                  _resident((ko, d)),
                  _resident((1, d)),
                  _resident((d, 2 * hidden)),
                  _resident((hidden, d))],
        out_specs=pl.BlockSpec((tm, d), lambda i: (i, 0)),
        out_shape=jax.ShapeDtypeStruct((n, d), F32),
        compiler_params=_cparams(("parallel",)),
        name="out_ffn",
    )(x, o, wmix, gain, win, wout)


def kernel(x, norm_mix, norm_ffn, gdn_w_in, gdn_conv_w, gdn_a_log, gdn_dt_bias, gdn_norm_w,
           gdn_w_out, dil_w_in, dil_q_norm, dil_k_norm, dil_w_out, ffn_w_in, ffn_w_out):
    batch, seq, d_model = x.shape
    depth = norm_mix.shape[0]
    n = batch * seq
    xf = x.reshape(n, d_model)
    gdn_main = 4 * N_HEADS * GDN_DK

    for i in range(depth):
        j = i // 2
        gain_mix = norm_mix[i].reshape(1, d_model)
        if i % 2 == 0:
            w_in = gdn_w_in[j]
            wab = jnp.zeros((d_model, LANES), BF16).at[:, :2 * N_HEADS].set(
                w_in[:, gdn_main:].astype(BF16))
            gconst = jnp.zeros((8, LANES), F32)
            gconst = gconst.at[0, :N_HEADS].set(-jnp.exp(gdn_a_log[j].astype(F32)))
            gconst = gconst.at[1, :N_HEADS].set(gdn_dt_bias[j].astype(F32))
            proj, colpack, gct = _gdn_proj(xf, gain_mix, w_in[:, :gdn_main].astype(BF16), wab, gconst)
            mixed = _gdn_chunk(proj, colpack, gct, gdn_conv_w[j].astype(F32),
                               gdn_norm_w[j].reshape(1, GDN_DK).astype(F32), batch=batch, seq=seq)
            w_mix = gdn_w_out[j]
        else:
            proj = _norm_proj(xf, gain_mix, dil_w_in[j].astype(BF16))
            qw = jnp.tile(dil_q_norm[j].astype(F32), N_HEADS).reshape(1, ATT_WIDTH)
            kw = jnp.tile(dil_k_norm[j].astype(F32), N_HEADS).reshape(1, ATT_WIDTH)
            prev = None
            for gi in range(len(ATT_GROUPS)):
                last = gi == len(ATT_GROUPS) - 1
                prev = _att_group(proj, qw, kw, prev, gi=gi, batch=batch, seq=seq, last=last)
            mixed = prev
            w_mix = dil_w_out[j]
        xf = _out_ffn(xf, mixed, w_mix.astype(BF16), norm_ffn[i].reshape(1, d_model),
                      ffn_w_in[i].astype(BF16), ffn_w_out[i].astype(BF16))
    return xf.reshape(batch, seq, d_model)
```

```python
import functools

import jax
import jax.numpy as jnp
from jax import lax
from jax.experimental import pallas as pl
from jax.experimental.pallas import tpu as pltpu

F32 = jnp.float32
BF16 = jnp.bfloat16

RMS_EPS = 1e-6
L2_EPS = 1e-6

N_HEADS = 8
GDN_DK = 128
GDN_CONV = 4
CHUNK = 64
INV_BASE = 8

ATT_GROUPS = ((128, 1), (512, 4), (2048, 16))
ATT_SPAN = 128
ATT_DH = 64
ATT_WIDTH = N_HEADS * ATT_DH
ALIBI_MAX_BIAS = 8.0

LANES = 128
VMEM_LIMIT = 56 * 1024 * 1024


def _cparams(sem):
    return pltpu.CompilerParams(dimension_semantics=sem, vmem_limit_bytes=VMEM_LIMIT)


def _resident(shape):
    nd = len(shape)
    return pl.BlockSpec(shape, lambda *_: (0,) * nd, pipeline_mode=pl.Buffered(1))


def _rms_scale(x, gain):
    ms = jnp.mean(x * x, axis=-1, keepdims=True)
    return x * lax.rsqrt(ms + RMS_EPS) * gain


def _sigmoid(x):
    return 1.0 / (1.0 + jnp.exp(-x))


def _dot(a, b):
    return jnp.dot(a, b, preferred_element_type=F32)


def _dot_nt(a, b):
    return lax.dot_general(a, b, (((1,), (1,)), ((), ())), preferred_element_type=F32)


def _dot_tn(a, b):
    return lax.dot_general(a, b, (((0,), (0,)), ((), ())), preferred_element_type=F32)


def _dil_proj_kernel(x_ref, gain_ref, w0_ref, w1_ref, w2_ref, p0_ref, p1_ref, p2_ref,
                     xn_ref, perm_ref):
    tm = x_ref.shape[0]
    xn = _rms_scale(x_ref[...], gain_ref[...])
    n_tiles = xn.shape[1] // LANES
    for c in range(n_tiles):
        xn_ref[c] = xn[:, c * LANES:(c + 1) * LANES]
    for (window, d), w_ref, p_ref in zip(ATT_GROUPS, (w0_ref, w1_ref, w2_ref),
                                         (p0_ref, p1_ref, p2_ref)):
        rows = tm // d
        if d == 1:
            lhs = xn.astype(BF16)
        else:
            for r in range(d):
                for c in range(n_tiles):
                    perm_ref[r * rows:(r + 1) * rows, c * LANES:(c + 1) * LANES] = (
                        xn_ref[c, pl.ds(r, rows, stride=d), :].astype(BF16))
            lhs = perm_ref[...]
        for j in range(3):
            cols = slice(j * ATT_WIDTH, (j + 1) * ATT_WIDTH)
            res = _dot(lhs, w_ref[:, cols]).astype(BF16)
            for r in range(d):
                p_ref[r, :, cols] = res[r * rows:(r + 1) * rows]


def _dil_proj(x, gain, ws, *, batch, seq, tm=512):
    n, d_model = x.shape
    nt = seq // tm
    out_specs, out_shape = [], []
    for window, d in ATT_GROUPS:
        out_specs.append(pl.BlockSpec((None, d, tm // d, 3 * ATT_WIDTH), lambda b, i: (b, 0, i, 0)))
        out_shape.append(jax.ShapeDtypeStruct((batch, d, seq // d, 3 * ATT_WIDTH), BF16))
    return pl.pallas_call(
        _dil_proj_kernel,
        grid=(batch, nt),
        in_specs=[pl.BlockSpec((tm, d_model), lambda b, i: (b * nt + i, 0)),
                  _resident((1, d_model))] + [_resident((d_model, 3 * ATT_WIDTH))] * 3,
        out_specs=out_specs,
        out_shape=out_shape,
        scratch_shapes=[pltpu.VMEM((d_model // LANES, tm, LANES), F32),
                        pltpu.VMEM((tm, d_model), BF16)],
        compiler_params=_cparams(("parallel", "parallel")),
        name="dil_proj",
    )(x, gain, *ws)


def _gdn_proj_kernel(x_ref, gain_ref, w_ref, wab_ref, gconst_ref, proj_ref, colpack_ref, gct_ref,
                     *, tn):
    tm = x_ref.shape[0]
    xn = _rms_scale(x_ref[...], gain_ref[...]).astype(BF16)
    for j in range(w_ref.shape[1] // tn):
        proj_ref[:, j * tn:(j + 1) * tn] = _dot(xn, w_ref[:, j * tn:(j + 1) * tn]).astype(BF16)

    ab = _dot(xn, wab_ref[...])
    lane = lax.broadcasted_iota(jnp.int32, (tm, LANES), 1)
    row = lax.broadcasted_iota(jnp.int32, (tm, LANES), 0)
    head_lanes = lane < N_HEADS
    neg_a = gconst_ref[0:1, :]
    dt_bias = gconst_ref[1:2, :]
    sp_in = ab + dt_bias
    softplus = jnp.maximum(sp_in, 0.0) + jnp.log(1.0 + jnp.exp(-jnp.abs(sp_in)))
    g = neg_a * softplus
    beta = _sigmoid(ab)

    rc = row & (CHUNK - 1)
    gc = g
    s = 1
    while s < CHUNK:
        gc = gc + jnp.where(rc >= s, pltpu.roll(gc, s, axis=0), 0.0)
        s *= 2
    gc3 = gc.reshape(tm // CHUNK, CHUNK, LANES)
    gl = jnp.broadcast_to(gc3[:, CHUNK - 1:CHUNK, :], gc3.shape).reshape(tm, LANES)

    eg = jnp.exp(gc)
    beta0 = pltpu.roll(beta, LANES - N_HEADS, axis=1)

    def put(v, at):
        v = jnp.where(head_lanes, v, 0.0)
        return pltpu.roll(v, at, axis=1) if at else v

    colpack_ref[...] = (put(gc, 0) + jnp.where((lane >= 8) & (lane < 16), beta, 0.0)
                        + put(eg, 16) + put(jnp.exp(gl - gc), 24) + put(jnp.exp(gl), 32)
                        + put(beta0 * eg, 40))

    sel = (lax.broadcasted_iota(jnp.int32, (N_HEADS, LANES), 0)
           == lax.broadcasted_iota(jnp.int32, (N_HEADS, LANES), 1)).astype(BF16)
    gcm = jnp.where(head_lanes, gc, 0.0)
    hi = gcm.astype(BF16)
    r1 = gcm - hi.astype(F32)
    mid = r1.astype(BF16)
    lo = (r1 - mid.astype(F32)).astype(BF16)
    for c in range(tm // CHUNK):
        sl = slice(c * CHUNK, (c + 1) * CHUNK)
        gct_ref[c] = (_dot_nt(sel, hi[sl]) + _dot_nt(sel, mid[sl])) + _dot_nt(sel, lo[sl])


def _gdn_proj(x, gain, w, wab, gconst, *, tm=512, tn=512):
    n, d = x.shape
    wout = w.shape[1]
    return pl.pallas_call(
        functools.partial(_gdn_proj_kernel, tn=tn),
        grid=(n // tm,),
        in_specs=[pl.BlockSpec((tm, d), lambda i: (i, 0)),
                  _resident((1, d)),
                  _resident((d, wout)),
                  _resident((d, LANES)),
                  _resident((8, LANES))],
        out_specs=[pl.BlockSpec((tm, wout), lambda i: (i, 0)),
                   pl.BlockSpec((tm, LANES), lambda i: (i, 0)),
                   pl.BlockSpec((tm // CHUNK, N_HEADS, CHUNK), lambda i: (i, 0, 0))],
        out_shape=[jax.ShapeDtypeStruct((n, wout), BF16),
                   jax.ShapeDtypeStruct((n, LANES), F32),
                   jax.ShapeDtypeStruct((n // CHUNK, N_HEADS, CHUNK), F32)],
        compiler_params=_cparams(("parallel",)),
        name="gdn_proj",
    )(x, gain, w, wab, gconst)


def _unit_lower_inverse(nms, masks):
    eye, m_base, level_masks = masks
    nbs = [jnp.where(m_base, nm, 0.0) for nm in nms]
    xs = [eye - nb for nb in nbs]
    nb16 = [nb.astype(BF16) for nb in nbs]
    ps = [_dot(a, a) for a in nb16]
    s = 2
    while s < INV_BASE:
        p16 = [p.astype(BF16) for p in ps]
        s *= 2
        if s < INV_BASE:
            both = [_dot(a, jnp.concatenate([a, x.astype(BF16)], axis=1)) for a, x in zip(p16, xs)]
            ps = [b[:, :CHUNK] for b in both]
            xs = [x + b[:, CHUNK:] for x, b in zip(xs, both)]
        else:
            xs = [x + _dot(a, x.astype(BF16)) for a, x in zip(p16, xs)]
    for m_off in level_masks:
        x16 = [x.astype(BF16) for x in xs]
        ws = [_dot(jnp.where(m_off, nm, 0.0).astype(BF16), a) for nm, a in zip(nms, x16)]
        xs = [x - _dot(a, w.astype(BF16)) for x, a, w in zip(xs, x16, ws)]
    return xs


def _gdn_chunk_kernel(q_ref, k_ref, v_ref, z_ref, colpack_ref, gct_ref, convw_ref, normw_ref,
                      o_ref, state_ref, halo_ref, qs_ref, ks_ref, vs_ref,
                      u_ref, w_ref, qd_ref, kt_ref, qk_ref, *, prep_chunks):
    t = pl.program_id(1)
    tt = q_ref.shape[0]
    width = N_HEADS * GDN_DK
    n_chunks = tt // CHUNK

    @pl.when(t == 0)
    def _():
        state_ref[...] = jnp.zeros_like(state_ref)
        halo_ref[...] = jnp.zeros_like(halo_ref)

    for idx, (src, dst) in enumerate(((q_ref, qs_ref), (k_ref, ks_ref), (v_ref, vs_ref))):
        for h in range(N_HEADS):
            cols = slice(h * GDN_DK, (h + 1) * GDN_DK)
            hcols = slice(idx * width + h * GDN_DK, idx * width + (h + 1) * GDN_DK)
            x = src[:, cols].astype(F32)
            xx = jnp.concatenate([halo_ref[:, hcols], x], axis=0)
            halo_ref[:, hcols] = x[tt - 8:, :]
            y = convw_ref[GDN_CONV - 1:GDN_CONV, hcols] * x
            for j in range(GDN_CONV - 1):
                off = 8 - (GDN_CONV - 1) + j
                y = y + convw_ref[j:j + 1, hcols] * xx[off:off + tt, :]
            y = y * _sigmoid(y)
            if idx < 2:
                y = y * lax.rsqrt(jnp.sum(y * y, axis=-1, keepdims=True) + L2_EPS)
            if idx == 0:
                y = y * (GDN_DK ** -0.5)
            dst[:, cols] = y

    ri = lax.broadcasted_iota(jnp.int32, (CHUNK, CHUNK), 0)
    ci = lax.broadcasted_iota(jnp.int32, (CHUNK, CHUNK), 1)
    causal = ri >= ci
    strict = ri > ci
    eye = (ri == ci).astype(F32)

    def same_block(sz):
        shift = sz.bit_length() - 1
        return (ri >> shift) == (ci >> shift)

    level_masks = []
    sz = INV_BASE
    while sz < CHUNK:
        level_masks.append(same_block(2 * sz) & jnp.logical_not(same_block(sz)))
        sz *= 2
    masks = (eye, same_block(INV_BASE), tuple(level_masks))
    normw = normw_ref[...]

    def hcols(h):
        return slice(h * GDN_DK, (h + 1) * GDN_DK)

    def crows(c):
        return slice(c * CHUNK, (c + 1) * CHUNK)

    for c0 in range(0, n_chunks, prep_chunks):
        pairs = [(c, h) for c in range(c0, c0 + prep_chunks) for h in range(N_HEADS)]
        cps = {c: colpack_ref[crows(c), :] for c in range(c0, c0 + prep_chunks)}
        gts = {c: gct_ref[c] for c in range(c0, c0 + prep_chunks)}
        qs = [qs_ref[crows(c), hcols(h)] for c, h in pairs]
        ks = [ks_ref[crows(c), hcols(h)] for c, h in pairs]
        k16 = [k.astype(BF16) for k in ks]
        qk_kk = [_dot_nt(jnp.concatenate([q.astype(BF16), kb], axis=0), kb)
                 for q, kb in zip(qs, k16)]
        decay = [jnp.exp(jnp.where(causal, cps[c][:, h:h + 1] - gts[c][h:h + 1, :], -jnp.inf))
                 for c, h in pairs]
        for (c, h), a, d in zip(pairs, qk_kk, decay):
            qk_ref[crows(c), h * CHUNK:(h + 1) * CHUNK] = (a[:CHUNK] * d).astype(BF16)
        nms = [jnp.where(strict, a[CHUNK:] * cps[c][:, 8 + h:9 + h] * d, 0.0)
               for (c, h), a, d in zip(pairs, qk_kk, decay)]
        tinv = _unit_lower_inverse(nms, masks)
        rhs = [jnp.concatenate([vs_ref[crows(c), hcols(h)] * cps[c][:, 8 + h:9 + h],
                                k * cps[c][:, 40 + h:41 + h]], axis=1).astype(BF16)
               for (c, h), k in zip(pairs, ks)]
        sol = [_dot(ti.astype(BF16), r) for ti, r in zip(tinv, rhs)]
        for (c, h), s_, q, k in zip(pairs, sol, qs, ks):
            u_ref[crows(c), hcols(h)] = s_[:, :GDN_DK]
            w_ref[crows(c), hcols(h)] = s_[:, GDN_DK:].astype(BF16)
            qd_ref[crows(c), hcols(h)] = (q * cps[c][:, 16 + h:17 + h]).astype(BF16)
            kt_ref[crows(c), hcols(h)] = (k * cps[c][:, 24 + h:25 + h]).astype(BF16)

    heads = range(N_HEADS)
    states = [state_ref[h] for h in heads]
    for c in range(n_chunks):
        rows = crows(c)
        s16 = [s_.astype(BF16) for s_ in states]
        ws_qs = [_dot(jnp.concatenate([w_ref[rows, hcols(h)], qd_ref[rows, hcols(h)]], axis=0),
                      s16[h]) for h in heads]
        vn16 = [(u_ref[rows, hcols(h)] - ws_qs[h][:CHUNK]).astype(BF16) for h in heads]
        outs = [ws_qs[h][CHUNK:] + _dot(qk_ref[rows, h * CHUNK:(h + 1) * CHUNK], vn16[h])
                for h in heads]
        elast = colpack_ref[c * CHUNK:c * CHUNK + 1, :]
        states = [states[h] * elast[:, 32 + h:33 + h] + _dot_tn(kt_ref[rows, hcols(h)], vn16[h])
                  for h in heads]
        for h in heads:
            o = outs[h]
            o = o * lax.rsqrt(jnp.mean(o * o, axis=-1, keepdims=True) + RMS_EPS)
            z = z_ref[rows, hcols(h)].astype(F32)
            o_ref[rows, hcols(h)] = (o * normw * (z * _sigmoid(z))).astype(o_ref.dtype)
    for h in heads:
        state_ref[h] = states[h]


def _gdn_chunk(proj, colpack, gct, conv_w, norm_w, *, batch, seq, tt=256, prep_chunks=4):
    n = batch * seq
    nt = seq // tt
    width = N_HEADS * GDN_DK

    def col_block(cb):
        return pl.BlockSpec((tt, width), lambda b, t: (b * nt + t, cb))

    return pl.pallas_call(
        functools.partial(_gdn_chunk_kernel, prep_chunks=prep_chunks),
        grid=(batch, nt),
        in_specs=[col_block(0), col_block(1), col_block(2), col_block(3),
                  pl.BlockSpec((tt, LANES), lambda b, t: (b * nt + t, 0)),
                  pl.BlockSpec((tt // CHUNK, N_HEADS, CHUNK), lambda b, t: (b * nt + t, 0, 0)),
                  _resident((GDN_CONV, 3 * width)),
                  _resident((1, GDN_DK))],
        out_specs=pl.BlockSpec((tt, width), lambda b, t: (b * nt + t, 0)),
        out_shape=jax.ShapeDtypeStruct((n, width), BF16),
        scratch_shapes=[pltpu.VMEM((N_HEADS, GDN_DK, GDN_DK), F32),
                        pltpu.VMEM((8, 3 * width), F32),
                        pltpu.VMEM((tt, width), F32),
                        pltpu.VMEM((tt, width), F32),
                        pltpu.VMEM((tt, width), F32),
                        pltpu.VMEM((tt, width), F32),
                        pltpu.VMEM((tt, width), BF16),
                        pltpu.VMEM((tt, width), BF16),
                        pltpu.VMEM((tt, width), BF16),
                        pltpu.VMEM((tt, N_HEADS * CHUNK), BF16)],
        compiler_params=_cparams(("parallel", "arbitrary")),
        name="gdn_chunk",
    )(proj, proj, proj, proj, colpack, gct, conv_w, norm_w)


def _att_group_kernel(q_ref, k_ref, v_ref, qw_ref, kw_ref, o_ref, lse_ref, kn_ref, vn_ref,
                      *, dilation, slopes, nq):
    n = pl.program_id(2)
    span = ATT_SPAN
    rows_per_step = nq * span

    @pl.when(n == 0)
    def _():
        kn_ref[0:span, :] = jnp.zeros((span, ATT_WIDTH), BF16)
        vn_ref[0:span, :] = jnp.zeros((span, ATT_WIDTH), BF16)

    lane = lax.broadcasted_iota(jnp.int32, (rows_per_step, LANES), 1)
    left = lane < ATT_DH

    def head_norm(x, w):
        sq = x * x
        s_left = jnp.sum(jnp.where(left, sq, 0.0), axis=-1, keepdims=True)
        s_all = jnp.sum(sq, axis=-1, keepdims=True)
        r_left = lax.rsqrt(s_left * (1.0 / ATT_DH) + RMS_EPS)
        r_right = lax.rsqrt((s_all - s_left) * (1.0 / ATT_DH) + RMS_EPS)
        return x * jnp.where(left, r_left, r_right) * w

    qn = []
    for p in range(N_HEADS // 2):
        cols = slice(p * LANES, (p + 1) * LANES)
        kn_ref[span:, cols] = head_norm(k_ref[:, cols].astype(F32), kw_ref[:, cols]).astype(BF16)
        qn.append(head_norm(q_ref[:, cols].astype(F32), qw_ref[:, cols] * (ATT_DH ** -0.5))
                  .astype(BF16))
    vn_ref[span:, :] = v_ref[...]

    qi = lax.broadcasted_iota(jnp.int32, (span, 2 * span), 0)
    kj = lax.broadcasted_iota(jnp.int32, (span, 2 * span), 1)
    steps = qi + span - kj
    band = (steps >= 0) & (steps <= span)
    dist = (steps * dilation).astype(F32)
    lane_s = lax.broadcasted_iota(jnp.int32, (span, LANES), 1)
    left_s = lane_s < ATT_DH

    first_key = jnp.where(n == 0, span, 0)
    for i in range(nq):
        valid = band & (kj >= first_key) if i == 0 else band
        qrows = slice(i * span, (i + 1) * span)
        krows = slice(i * span, (i + 2) * span)
        lse_tile = jnp.zeros((span, LANES), F32)
        for p in range(N_HEADS // 2):
            cols = slice(p * LANES, (p + 1) * LANES)
            kcat = kn_ref[krows, cols]
            vcat = vn_ref[krows, cols]
            q_pair = qn[p][qrows]
            halves = []
            for e in range(2):
                h = 2 * p + e
                qm = jnp.where(left_s if e == 0 else jnp.logical_not(left_s), q_pair,
                               jnp.zeros_like(q_pair))
                s = _dot_nt(qm, kcat) - slopes[h] * dist
                s = jnp.where(valid, s, -jnp.inf)
                m = jnp.max(s, axis=-1, keepdims=True)
                pr = jnp.exp(s - m)
                l = jnp.sum(pr, axis=-1, keepdims=True)
                halves.append(_dot(pr.astype(BF16), vcat) / l)
                lse_tile = jnp.where(lane_s == h, m + jnp.log(l), lse_tile)
            o_ref[qrows, cols] = jnp.where(left_s, halves[0], halves[1])
        lse_ref[qrows, :] = lse_tile

    kn_ref[0:span, :] = kn_ref[rows_per_step:, :]
    vn_ref[0:span, :] = vn_ref[rows_per_step:, :]


def _att_group(proj_g, qw, kw, *, gi, batch, seq):
    window, d = ATT_GROUPS[gi]
    assert window // d == ATT_SPAN
    n_groups = len(ATT_GROUPS)
    sub = seq // d
    nblk = sub // ATT_SPAN
    nq = min(4, nblk)
    rows = nq * ATT_SPAN
    slopes = tuple(2.0 ** (-ALIBI_MAX_BIAS * (gi * N_HEADS + h + 1) / (n_groups * N_HEADS))
                   for h in range(N_HEADS))

    def proj_block(which):
        return pl.BlockSpec((None, None, rows, ATT_WIDTH), lambda b, r, n: (b, r, n, which))

    return pl.pallas_call(
        functools.partial(_att_group_kernel, dilation=d, slopes=slopes, nq=nq),
        grid=(batch, d, nblk // nq),
        in_specs=[proj_block(0), proj_block(1), proj_block(2),
                  _resident((1, ATT_WIDTH)), _resident((1, ATT_WIDTH))],
        out_specs=[pl.BlockSpec((None, None, rows, ATT_WIDTH), lambda b, r, n: (b, r, n, 0)),
                   pl.BlockSpec((None, None, rows, LANES), lambda b, r, n: (b, r, n, 0))],
        out_shape=[jax.ShapeDtypeStruct((batch, d, sub, ATT_WIDTH), F32),
                   jax.ShapeDtypeStruct((batch, d, sub, LANES), F32)],
        scratch_shapes=[pltpu.VMEM((ATT_SPAN + rows, ATT_WIDTH), BF16),
                        pltpu.VMEM((ATT_SPAN + rows, ATT_WIDTH), BF16)],
        compiler_params=_cparams(("parallel", "parallel", "arbitrary")),
        name=f"att_group{gi}",
    )(proj_g, proj_g, proj_g, qw, kw)


def _ffn_tail(x, mixed16, wmix_ref, gain_ref, win_ref, wout_ref, y_ref, th):
    hidden = wout_ref.shape[0]
    x1 = x + _dot(mixed16, wmix_ref[...])
    xn = _rms_scale(x1, gain_ref[...]).astype(BF16)
    acc = x1
    for j in range(hidden // th):
        gate = _dot(xn, win_ref[:, j * th:(j + 1) * th])
        up = _dot(xn, win_ref[:, hidden + j * th:hidden + (j + 1) * th])
        act = (gate * _sigmoid(gate) * up).astype(BF16)
        acc = acc + _dot(act, wout_ref[j * th:(j + 1) * th, :])
    y_ref[...] = acc


def _out_ffn_kernel(x_ref, o_ref, wmix_ref, gain_ref, win_ref, wout_ref, y_ref, *, th):
    _ffn_tail(x_ref[...], o_ref[...], wmix_ref, gain_ref, win_ref, wout_ref, y_ref, th)


def _out_ffn(x, o, wmix, gain, win, wout, *, tm=512, th=1408):
    n, d = x.shape
    ko = o.shape[1]
    hidden = wout.shape[0]
    return pl.pallas_call(
        functools.partial(_out_ffn_kernel, th=th),
        grid=(n // tm,),
        in_specs=[pl.BlockSpec((tm, d), lambda i: (i, 0)),
                  pl.BlockSpec((tm, ko), lambda i: (i, 0)),
                  _resident((ko, d)),
                  _resident((1, d)),
                  _resident((d, 2 * hidden)),
                  _resident((hidden, d))],
        out_specs=pl.BlockSpec((tm, d), lambda i: (i, 0)),
        out_shape=jax.ShapeDtypeStruct((n, d), F32),
        compiler_params=_cparams(("parallel",)),
        name="out_ffn",
    )(x, o, wmix, gain, win, wout)


def _merge_out_ffn_kernel(x_ref, o0_ref, o1_ref, o2_ref, l0_ref, l1_ref, l2_ref,
                          wmix_ref, gain_ref, win_ref, wout_ref, y_ref, onat_ref, lnat_ref, *, th):
    tm = x_ref.shape[0]
    n_pairs = N_HEADS // 2
    outs, lses = [], []
    for g, ((window, d), o_ref, l_ref) in enumerate(zip(ATT_GROUPS, (o0_ref, o1_ref, o2_ref),
                                                        (l0_ref, l1_ref, l2_ref))):
        if d == 1:
            outs.append([o_ref[0, :, p * LANES:(p + 1) * LANES] for p in range(n_pairs)])
            lses.append(l_ref[0])
            continue
        rows = tm // d
        for r in range(d):
            for p in range(n_pairs):
                onat_ref[g - 1, p, pl.ds(r, rows, stride=d), :] = (
                    o_ref[r, :, p * LANES:(p + 1) * LANES])
            lnat_ref[g - 1, pl.ds(r, rows, stride=d), :] = l_ref[r]
        outs.append([onat_ref[g - 1, p] for p in range(n_pairs)])
        lses.append(lnat_ref[g - 1])
    mx = jnp.maximum(jnp.maximum(lses[0], lses[1]), lses[2])
    wts = [jnp.exp(l - mx) for l in lses]
    inv = 1.0 / (wts[0] + wts[1] + wts[2])
    wts = [w * inv for w in wts]
    left = lax.broadcasted_iota(jnp.int32, (tm, LANES), 1) < ATT_DH
    pairs = []
    for p in range(n_pairs):
        acc = None
        for w, o in zip(wts, outs):
            term = jnp.where(left, w[:, 2 * p:2 * p + 1], w[:, 2 * p + 1:2 * p + 2]) * o[p]
            acc = term if acc is None else acc + term
        pairs.append(acc.astype(BF16))
    mixed16 = jnp.concatenate(pairs, axis=1)
    _ffn_tail(x_ref[...], mixed16, wmix_ref, gain_ref, win_ref, wout_ref, y_ref, th)


def _merge_out_ffn(x, outs, lses, wmix, gain, win, wout, *, batch, seq, tm=512, th=1408):
    n, d_model = x.shape
    nt = seq // tm
    hidden = wout.shape[0]
    o_specs = [pl.BlockSpec((None, d, tm // d, ATT_WIDTH), lambda b, i: (b, 0, i, 0))
               for window, d in ATT_GROUPS]
    l_specs = [pl.BlockSpec((None, d, tm // d, LANES), lambda b, i: (b, 0, i, 0))
               for window, d in ATT_GROUPS]
    return pl.pallas_call(
        functools.partial(_merge_out_ffn_kernel, th=th),
        grid=(batch, nt),
        in_specs=[pl.BlockSpec((tm, d_model), lambda b, i: (b * nt + i, 0))] + o_specs + l_specs
                 + [_resident((ATT_WIDTH, d_model)), _resident((1, d_model)),
                    _resident((d_model, 2 * hidden)), _resident((hidden, d_model))],
        out_specs=pl.BlockSpec((tm, d_model), lambda b, i: (b * nt + i, 0)),
        out_shape=jax.ShapeDtypeStruct((n, d_model), F32),
        scratch_shapes=[pltpu.VMEM((len(ATT_GROUPS) - 1, ATT_WIDTH // LANES, tm, LANES), F32),
                        pltpu.VMEM((len(ATT_GROUPS) - 1, tm, LANES), F32)],
        compiler_params=_cparams(("parallel", "parallel")),
        name="merge_out_ffn",
    )(x, *outs, *lses, wmix, gain, win, wout)


def _gdn_layer(xf, gain_mix, w_in, conv_w, a_log, dt_bias, norm_w, *, batch, seq):
    d_model = xf.shape[1]
    main = 4 * N_HEADS * GDN_DK
    wab = jnp.zeros((d_model, LANES), BF16).at[:, :2 * N_HEADS].set(w_in[:, main:].astype(BF16))
    gconst = jnp.zeros((8, LANES), F32)
    gconst = gconst.at[0, :N_HEADS].set(-jnp.exp(a_log.astype(F32)))
    gconst = gconst.at[1, :N_HEADS].set(dt_bias.astype(F32))
    proj, colpack, gct = _gdn_proj(xf, gain_mix, w_in[:, :main].astype(BF16), wab, gconst)
    return _gdn_chunk(proj, colpack, gct, conv_w.astype(F32),
                      norm_w.reshape(1, GDN_DK).astype(F32), batch=batch, seq=seq)


def _att_layer(xf, gain_mix, w_in, q_norm, k_norm, *, batch, seq):
    n_groups = len(ATT_GROUPS)
    w_in = w_in.astype(BF16)
    ws = [jnp.concatenate([w_in[:, (which * n_groups + gi) * ATT_WIDTH:
                                (which * n_groups + gi + 1) * ATT_WIDTH]
                           for which in range(3)], axis=1) for gi in range(n_groups)]
    projs = _dil_proj(xf, gain_mix, ws, batch=batch, seq=seq)
    qw = jnp.tile(q_norm.astype(F32), N_HEADS).reshape(1, ATT_WIDTH)
    kw = jnp.tile(k_norm.astype(F32), N_HEADS).reshape(1, ATT_WIDTH)
    res = [_att_group(projs[gi], qw, kw, gi=gi, batch=batch, seq=seq) for gi in range(n_groups)]
    return [r[0] for r in res], [r[1] for r in res]


def kernel(x, norm_mix, norm_ffn, gdn_w_in, gdn_conv_w, gdn_a_log, gdn_dt_bias, gdn_norm_w,
           gdn_w_out, dil_w_in, dil_q_norm, dil_k_norm, dil_w_out, ffn_w_in, ffn_w_out):
    batch, seq, d_model = x.shape
    depth = norm_mix.shape[0]
    xf = x.reshape(batch * seq, d_model)
    for i in range(depth):
        j = i // 2
        gain_mix = norm_mix[i].reshape(1, d_model)
        ffn = (norm_ffn[i].reshape(1, d_model), ffn_w_in[i].astype(BF16), ffn_w_out[i].astype(BF16))
        if i % 2 == 0:
            mixed = _gdn_layer(xf, gain_mix, gdn_w_in[j], gdn_conv_w[j], gdn_a_log[j],
                               gdn_dt_bias[j], gdn_norm_w[j], batch=batch, seq=seq)
            xf = _out_ffn(xf, mixed, gdn_w_out[j].astype(BF16), *ffn)
        else:
            outs, lses = _att_layer(xf, gain_mix, dil_w_in[j], dil_q_norm[j], dil_k_norm[j],
                                    batch=batch, seq=seq)
            xf = _merge_out_ffn(xf, outs, lses, dil_w_out[j].astype(BF16), *ffn,
                                batch=batch, seq=seq)
    return xf.reshape(batch, seq, d_model)
```

```python
import functools

import jax
import jax.numpy as jnp
from jax import lax
from jax.experimental import pallas as pl
from jax.experimental.pallas import tpu as pltpu

F32 = jnp.float32
BF16 = jnp.bfloat16

RMS_EPS = 1e-6
L2_EPS = 1e-6

N_HEADS = 8
GDN_DK = 128
GDN_CONV = 4
CHUNK = 64
INV_BASE = 8

ATT_GROUPS = ((128, 1), (512, 4), (2048, 16))
ATT_SPAN = 128
ATT_DH = 64
ATT_WIDTH = N_HEADS * ATT_DH
ALIBI_MAX_BIAS = 8.0

LANES = 128
VMEM_LIMIT = 56 * 1024 * 1024


def _cparams(sem):
    return pltpu.CompilerParams(dimension_semantics=sem, vmem_limit_bytes=VMEM_LIMIT)


def _resident(shape):
    nd = len(shape)
    return pl.BlockSpec(shape, lambda *_: (0,) * nd, pipeline_mode=pl.Buffered(1))


def _rms_scale(x, gain):
    ms = jnp.mean(x * x, axis=-1, keepdims=True)
    return x * lax.rsqrt(ms + RMS_EPS) * gain


def _sigmoid(x):
    return 1.0 / (1.0 + jnp.exp(-x))


def _dot(a, b):
    return jnp.dot(a, b, preferred_element_type=F32)


def _dot_nt(a, b):
    return lax.dot_general(a, b, (((1,), (1,)), ((), ())), preferred_element_type=F32)


def _dot_tn(a, b):
    return lax.dot_general(a, b, (((0,), (0,)), ((), ())), preferred_element_type=F32)


def _dil_proj_kernel(x_ref, gain_ref, w0_ref, w1_ref, w2_ref, p0_ref, p1_ref, p2_ref,
                     xn_ref, perm_ref):
    tm = x_ref.shape[0]
    xn = _rms_scale(x_ref[...], gain_ref[...])
    n_tiles = xn.shape[1] // LANES
    for c in range(n_tiles):
        xn_ref[c] = xn[:, c * LANES:(c + 1) * LANES]
    for (window, d), w_ref, p_ref in zip(ATT_GROUPS, (w0_ref, w1_ref, w2_ref),
                                         (p0_ref, p1_ref, p2_ref)):
        rows = tm // d
        if d == 1:
            lhs = xn.astype(BF16)
        else:
            for r in range(d):
                for c in range(n_tiles):
                    perm_ref[r * rows:(r + 1) * rows, c * LANES:(c + 1) * LANES] = (
                        xn_ref[c, pl.ds(r, rows, stride=d), :].astype(BF16))
            lhs = perm_ref[...]
        for j in range(3):
            cols = slice(j * ATT_WIDTH, (j + 1) * ATT_WIDTH)
            res = _dot(lhs, w_ref[:, cols]).astype(BF16)
            for r in range(d):
                p_ref[r, :, cols] = res[r * rows:(r + 1) * rows]


def _dil_proj(x, gain, ws, *, batch, seq, tm=512):
    n, d_model = x.shape
    nt = seq // tm
    out_specs, out_shape = [], []
    for window, d in ATT_GROUPS:
        out_specs.append(pl.BlockSpec((None, d, tm // d, 3 * ATT_WIDTH), lambda b, i: (b, 0, i, 0)))
        out_shape.append(jax.ShapeDtypeStruct((batch, d, seq // d, 3 * ATT_WIDTH), BF16))
    return pl.pallas_call(
        _dil_proj_kernel,
        grid=(batch, nt),
        in_specs=[pl.BlockSpec((tm, d_model), lambda b, i: (b * nt + i, 0)),
                  _resident((1, d_model))] + [_resident((d_model, 3 * ATT_WIDTH))] * 3,
        out_specs=out_specs,
        out_shape=out_shape,
        scratch_shapes=[pltpu.VMEM((d_model // LANES, tm, LANES), F32),
                        pltpu.VMEM((tm, d_model), BF16)],
        compiler_params=_cparams(("parallel", "parallel")),
        name="dil_proj",
    )(x, gain, *ws)


def _gdn_proj_kernel(x_ref, gain_ref, w_ref, wab_ref, gconst_ref, convw_ref,
                     proj_ref, colpack_ref, gct_ref, halo_ref, *, tn):
    tm = x_ref.shape[0]
    width = N_HEADS * GDN_DK

    @pl.when(pl.program_id(1) == 0)
    def _():
        halo_ref[...] = jnp.zeros_like(halo_ref)

    xn = _rms_scale(x_ref[...], gain_ref[...]).astype(BF16)
    for j in range(w_ref.shape[1] // tn):
        cols = slice(j * tn, (j + 1) * tn)
        res = _dot(xn, w_ref[:, cols])
        if j * tn >= 3 * width:
            proj_ref[:, cols] = res.astype(BF16)
            continue
        hist = jnp.concatenate([halo_ref[:, cols], res], axis=0)
        halo_ref[:, cols] = res[tm - 8:, :]
        y = convw_ref[GDN_CONV - 1:GDN_CONV, cols] * res
        for tap in range(GDN_CONV - 1):
            off = 8 - (GDN_CONV - 1) + tap
            y = y + convw_ref[tap:tap + 1, cols] * hist[off:off + tm, :]
        y = y * _sigmoid(y)
        for h in range(tn // GDN_DK):
            hc = slice(h * GDN_DK, (h + 1) * GDN_DK)
            yh = y[:, hc]
            if j * tn < 2 * width:
                inv = lax.rsqrt(jnp.sum(yh * yh, axis=-1, keepdims=True) + L2_EPS)
                yh = yh * (inv * (GDN_DK ** -0.5) if j * tn < width else inv)
            proj_ref[:, j * tn + h * GDN_DK:j * tn + (h + 1) * GDN_DK] = yh.astype(BF16)

    ab = _dot(xn, wab_ref[...])
    lane = lax.broadcasted_iota(jnp.int32, (tm, LANES), 1)
    row = lax.broadcasted_iota(jnp.int32, (tm, LANES), 0)
    head_lanes = lane < N_HEADS
    neg_a = gconst_ref[0:1, :]
    dt_bias = gconst_ref[1:2, :]
    sp_in = ab + dt_bias
    softplus = jnp.maximum(sp_in, 0.0) + jnp.log(1.0 + jnp.exp(-jnp.abs(sp_in)))
    g = neg_a * softplus
    beta = _sigmoid(ab)

    rc = row & (CHUNK - 1)
    gc = g
    s = 1
    while s < CHUNK:
        gc = gc + jnp.where(rc >= s, pltpu.roll(gc, s, axis=0), 0.0)
        s *= 2
    gc3 = gc.reshape(tm // CHUNK, CHUNK, LANES)
    gl = jnp.broadcast_to(gc3[:, CHUNK - 1:CHUNK, :], gc3.shape).reshape(tm, LANES)

    eg = jnp.exp(gc)
    beta0 = pltpu.roll(beta, LANES - N_HEADS, axis=1)

    def put(v, at):
        v = jnp.where(head_lanes, v, 0.0)
        return pltpu.roll(v, at, axis=1) if at else v

    colpack_ref[...] = (put(gc, 0) + jnp.where((lane >= 8) & (lane < 16), beta, 0.0)
                        + put(eg, 16) + put(jnp.exp(gl - gc), 24) + put(jnp.exp(gl), 32)
                        + put(beta0 * eg, 40))

    sel = (lax.broadcasted_iota(jnp.int32, (N_HEADS, LANES), 0)
           == lax.broadcasted_iota(jnp.int32, (N_HEADS, LANES), 1)).astype(BF16)
    gcm = jnp.where(head_lanes, gc, 0.0)
    hi = gcm.astype(BF16)
    r1 = gcm - hi.astype(F32)
    mid = r1.astype(BF16)
    lo = (r1 - mid.astype(F32)).astype(BF16)
    for c in range(tm // CHUNK):
        sl = slice(c * CHUNK, (c + 1) * CHUNK)
        gct_ref[c] = (_dot_nt(sel, hi[sl]) + _dot_nt(sel, mid[sl])) + _dot_nt(sel, lo[sl])


def _gdn_proj(x, gain, w, wab, gconst, conv_w, *, batch, seq, tm=512, tn=512):
    n, d = x.shape
    wout = w.shape[1]
    nt = seq // tm
    qkv = conv_w.shape[1]

    def rows(b, i):
        return (b * nt + i, 0)

    return pl.pallas_call(
        functools.partial(_gdn_proj_kernel, tn=tn),
        grid=(batch, nt),
        in_specs=[pl.BlockSpec((tm, d), rows),
                  _resident((1, d)),
                  _resident((d, wout)),
                  _resident((d, LANES)),
                  _resident((8, LANES)),
                  _resident((GDN_CONV, qkv))],
        out_specs=[pl.BlockSpec((tm, wout), rows),
                   pl.BlockSpec((tm, LANES), rows),
                   pl.BlockSpec((tm // CHUNK, N_HEADS, CHUNK), lambda b, i: (b * nt + i, 0, 0))],
        out_shape=[jax.ShapeDtypeStruct((n, wout), BF16),
                   jax.ShapeDtypeStruct((n, LANES), F32),
                   jax.ShapeDtypeStruct((n // CHUNK, N_HEADS, CHUNK), F32)],
        scratch_shapes=[pltpu.VMEM((8, qkv), F32)],
        compiler_params=_cparams(("parallel", "arbitrary")),
        name="gdn_proj",
    )(x, gain, w, wab, gconst, conv_w)


def _unit_lower_inverse(nms, masks):
    eye, m_base, level_masks = masks
    nbs = [jnp.where(m_base, nm, 0.0) for nm in nms]
    xs = [eye - nb for nb in nbs]
    nb16 = [nb.astype(BF16) for nb in nbs]
    ps = [_dot(a, a) for a in nb16]
    s = 2
    while s < INV_BASE:
        p16 = [p.astype(BF16) for p in ps]
        s *= 2
        if s < INV_BASE:
            both = [_dot(a, jnp.concatenate([a, x.astype(BF16)], axis=1)) for a, x in zip(p16, xs)]
            ps = [b[:, :CHUNK] for b in both]
            xs = [x + b[:, CHUNK:] for x, b in zip(xs, both)]
        else:
            xs = [x + _dot(a, x.astype(BF16)) for a, x in zip(p16, xs)]
    for m_off in level_masks:
        x16 = [x.astype(BF16) for x in xs]
        ws = [_dot(jnp.where(m_off, nm, 0.0).astype(BF16), a) for nm, a in zip(nms, x16)]
        xs = [x - _dot(a, w.astype(BF16)) for x, a, w in zip(xs, x16, ws)]
    return xs


def _gdn_chunk_kernel(q_ref, k_ref, v_ref, z_ref, colpack_ref, gct_ref, normw_ref,
                      o_ref, state_ref, u_ref, w_ref, qd_ref, kt_ref, qk_ref, *, prep_chunks):
    t = pl.program_id(1)
    tt = q_ref.shape[0]
    n_chunks = tt // CHUNK

    @pl.when(t == 0)
    def _():
        state_ref[...] = jnp.zeros_like(state_ref)

    ri = lax.broadcasted_iota(jnp.int32, (CHUNK, CHUNK), 0)
    ci = lax.broadcasted_iota(jnp.int32, (CHUNK, CHUNK), 1)
    causal = ri >= ci
    strict = ri > ci
    eye = (ri == ci).astype(F32)

    def same_block(sz):
        shift = sz.bit_length() - 1
        return (ri >> shift) == (ci >> shift)

    level_masks = []
    sz = INV_BASE
    while sz < CHUNK:
        level_masks.append(same_block(2 * sz) & jnp.logical_not(same_block(sz)))
        sz *= 2
    masks = (eye, same_block(INV_BASE), tuple(level_masks))
    normw = normw_ref[...]

    def hcols(h):
        return slice(h * GDN_DK, (h + 1) * GDN_DK)

    def crows(c):
        return slice(c * CHUNK, (c + 1) * CHUNK)

    for c0 in range(0, n_chunks, prep_chunks):
        pairs = [(c, h) for c in range(c0, c0 + prep_chunks) for h in range(N_HEADS)]
        cps = {c: colpack_ref[crows(c), :] for c in range(c0, c0 + prep_chunks)}
        gts = {c: gct_ref[c] for c in range(c0, c0 + prep_chunks)}
        q16 = [q_ref[crows(c), hcols(h)] for c, h in pairs]
        k16 = [k_ref[crows(c), hcols(h)] for c, h in pairs]
        qk_kk = [_dot_nt(jnp.concatenate([qb, kb], axis=0), kb)
                 for qb, kb in zip(q16, k16)]
        decay = [jnp.exp(jnp.where(causal, cps[c][:, h:h + 1] - gts[c][h:h + 1, :], -jnp.inf))
                 for c, h in pairs]
        for (c, h), a, d in zip(pairs, qk_kk, decay):
            qk_ref[crows(c), h * CHUNK:(h + 1) * CHUNK] = (a[:CHUNK] * d).astype(BF16)
        nms = [jnp.where(strict, a[CHUNK:] * cps[c][:, 8 + h:9 + h] * d, 0.0)
               for (c, h), a, d in zip(pairs, qk_kk, decay)]
        tinv = _unit_lower_inverse(nms, masks)
        qs = [qb.astype(F32) for qb in q16]
        ks = [kb.astype(F32) for kb in k16]
        rhs = [jnp.concatenate([v_ref[crows(c), hcols(h)].astype(F32) * cps[c][:, 8 + h:9 + h],
                                k * cps[c][:, 40 + h:41 + h]], axis=1).astype(BF16)
               for (c, h), k in zip(pairs, ks)]
        sol = [_dot(ti.astype(BF16), r) for ti, r in zip(tinv, rhs)]
        for (c, h), s_, q, k in zip(pairs, sol, qs, ks):
            u_ref[crows(c), hcols(h)] = s_[:, :GDN_DK]
            w_ref[crows(c), hcols(h)] = s_[:, GDN_DK:].astype(BF16)
            qd_ref[crows(c), hcols(h)] = (q * cps[c][:, 16 + h:17 + h]).astype(BF16)
            kt_ref[crows(c), hcols(h)] = (k * cps[c][:, 24 + h:25 + h]).astype(BF16)

    heads = range(N_HEADS)
    states = [state_ref[h] for h in heads]
    for c in range(n_chunks):
        rows = crows(c)
        s16 = [s_.astype(BF16) for s_ in states]
        ws_qs = [_dot(jnp.concatenate([w_ref[rows, hcols(h)], qd_ref[rows, hcols(h)]], axis=0),
                      s16[h]) for h in heads]
        vn16 = [(u_ref[rows, hcols(h)] - ws_qs[h][:CHUNK]).astype(BF16) for h in heads]
        outs = [ws_qs[h][CHUNK:] + _dot(qk_ref[rows, h * CHUNK:(h + 1) * CHUNK], vn16[h])
                for h in heads]
        elast = colpack_ref[c * CHUNK:c * CHUNK + 1, :]
        states = [states[h] * elast[:, 32 + h:33 + h] + _dot_tn(kt_ref[rows, hcols(h)], vn16[h])
                  for h in heads]
        for h in heads:
            o = outs[h]
            o = o * lax.rsqrt(jnp.mean(o * o, axis=-1, keepdims=True) + RMS_EPS)
            z = z_ref[rows, hcols(h)].astype(F32)
            o_ref[rows, hcols(h)] = (o * normw * (z * _sigmoid(z))).astype(o_ref.dtype)
    for h in heads:
        state_ref[h] = states[h]


def _gdn_chunk(proj, colpack, gct, norm_w, *, batch, seq, tt=256, prep_chunks=4):
    n = batch * seq
    nt = seq // tt
    width = N_HEADS * GDN_DK

    def col_block(cb):
        return pl.BlockSpec((tt, width), lambda b, t: (b * nt + t, cb))

    return pl.pallas_call(
        functools.partial(_gdn_chunk_kernel, prep_chunks=prep_chunks),
        grid=(batch, nt),
        in_specs=[col_block(0), col_block(1), col_block(2), col_block(3),
                  pl.BlockSpec((tt, LANES), lambda b, t: (b * nt + t, 0)),
                  pl.BlockSpec((tt // CHUNK, N_HEADS, CHUNK), lambda b, t: (b * nt + t, 0, 0)),
                  _resident((1, GDN_DK))],
        out_specs=pl.BlockSpec((tt, width), lambda b, t: (b * nt + t, 0)),
        out_shape=jax.ShapeDtypeStruct((n, width), BF16),
        scratch_shapes=[pltpu.VMEM((N_HEADS, GDN_DK, GDN_DK), F32),
                        pltpu.VMEM((tt, width), F32),
                        pltpu.VMEM((tt, width), BF16),
                        pltpu.VMEM((tt, width), BF16),
                        pltpu.VMEM((tt, width), BF16),
                        pltpu.VMEM((tt, N_HEADS * CHUNK), BF16)],
        compiler_params=_cparams(("parallel", "arbitrary")),
        name="gdn_chunk",
    )(proj, proj, proj, proj, colpack, gct, norm_w)


def _att_group_kernel(q_ref, k_ref, v_ref, qw_ref, kw_ref, o_ref, lse_ref, kn_ref, vn_ref,
                      *, dilation, slopes, nq):
    n = pl.program_id(2)
    span = ATT_SPAN
    rows_per_step = nq * span

    @pl.when(n == 0)
    def _():
        kn_ref[0:span, :] = jnp.zeros((span, ATT_WIDTH), BF16)
        vn_ref[0:span, :] = jnp.zeros((span, ATT_WIDTH), BF16)

    lane = lax.broadcasted_iota(jnp.int32, (rows_per_step, LANES), 1)
    left = lane < ATT_DH

    def head_norm(x, w):
        sq = x * x
        s_left = jnp.sum(jnp.where(left, sq, 0.0), axis=-1, keepdims=True)
        s_all = jnp.sum(sq, axis=-1, keepdims=True)
        r_left = lax.rsqrt(s_left * (1.0 / ATT_DH) + RMS_EPS)
        r_right = lax.rsqrt((s_all - s_left) * (1.0 / ATT_DH) + RMS_EPS)
        return x * jnp.where(left, r_left, r_right) * w

    qn = []
    for p in range(N_HEADS // 2):
        cols = slice(p * LANES, (p + 1) * LANES)
        kn_ref[span:, cols] = head_norm(k_ref[:, cols].astype(F32), kw_ref[:, cols]).astype(BF16)
        qn.append(head_norm(q_ref[:, cols].astype(F32), qw_ref[:, cols] * (ATT_DH ** -0.5))
                  .astype(BF16))
    vn_ref[span:, :] = v_ref[...]

    qi = lax.broadcasted_iota(jnp.int32, (span, 2 * span), 0)
    kj = lax.broadcasted_iota(jnp.int32, (span, 2 * span), 1)
    steps = qi + span - kj
    band = (steps >= 0) & (steps <= span)
    dist = (steps * dilation).astype(F32)
    lane_s = lax.broadcasted_iota(jnp.int32, (span, LANES), 1)
    left_s = lane_s < ATT_DH

    first_key = jnp.where(n == 0, span, 0)
    for i in range(nq):
        valid = band & (kj >= first_key) if i == 0 else band
        qrows = slice(i * span, (i + 1) * span)
        krows = slice(i * span, (i + 2) * span)
        lse_tile = jnp.zeros((span, LANES), F32)
        for p in range(N_HEADS // 2):
            cols = slice(p * LANES, (p + 1) * LANES)
            kcat = kn_ref[krows, cols]
            vcat = vn_ref[krows, cols]
            q_pair = qn[p][qrows]
            halves = []
            for e in range(2):
                h = 2 * p + e
                qm = jnp.where(left_s if e == 0 else jnp.logical_not(left_s), q_pair,
                               jnp.zeros_like(q_pair))
                s = _dot_nt(qm, kcat) - slopes[h] * dist
                s = jnp.where(valid, s, -jnp.inf)
                m = jnp.max(s, axis=-1, keepdims=True)
                pr = jnp.exp(s - m)
                l = jnp.sum(pr, axis=-1, keepdims=True)
                halves.append(_dot(pr.astype(BF16), vcat) / l)
                lse_tile = jnp.where(lane_s == h, m + jnp.log(l), lse_tile)
            o_ref[qrows, cols] = jnp.where(left_s, halves[0], halves[1])
        lse_ref[qrows, :] = lse_tile

    kn_ref[0:span, :] = kn_ref[rows_per_step:, :]
    vn_ref[0:span, :] = vn_ref[rows_per_step:, :]


def _att_group(proj_g, qw, kw, *, gi, batch, seq):
    window, d = ATT_GROUPS[gi]
    assert window // d == ATT_SPAN
    n_groups = len(ATT_GROUPS)
    sub = seq // d
    nblk = sub // ATT_SPAN
    nq = min(4, nblk)
    rows = nq * ATT_SPAN
    slopes = tuple(2.0 ** (-ALIBI_MAX_BIAS * (gi * N_HEADS + h + 1) / (n_groups * N_HEADS))
                   for h in range(N_HEADS))

    def proj_block(which):
        return pl.BlockSpec((None, None, rows, ATT_WIDTH), lambda b, r, n: (b, r, n, which))

    return pl.pallas_call(
        functools.partial(_att_group_kernel, dilation=d, slopes=slopes, nq=nq),
        grid=(batch, d, nblk // nq),
        in_specs=[proj_block(0), proj_block(1), proj_block(2),
                  _resident((1, ATT_WIDTH)), _resident((1, ATT_WIDTH))],
        out_specs=[pl.BlockSpec((None, None, rows, ATT_WIDTH), lambda b, r, n: (b, r, n, 0)),
                   pl.BlockSpec((None, None, rows, LANES), lambda b, r, n: (b, r, n, 0))],
        out_shape=[jax.ShapeDtypeStruct((batch, d, sub, ATT_WIDTH), F32),
                   jax.ShapeDtypeStruct((batch, d, sub, LANES), F32)],
        scratch_shapes=[pltpu.VMEM((ATT_SPAN + rows, ATT_WIDTH), BF16),
                        pltpu.VMEM((ATT_SPAN + rows, ATT_WIDTH), BF16)],
        compiler_params=_cparams(("parallel", "parallel", "arbitrary")),
        name=f"att_group{gi}",
    )(proj_g, proj_g, proj_g, qw, kw)


def _ffn_tail(x, mixed16, wmix_ref, gain_ref, win_ref, wout_ref, y_ref, th):
    hidden = wout_ref.shape[0]
    x1 = x + _dot(mixed16, wmix_ref[...])
    xn = _rms_scale(x1, gain_ref[...]).astype(BF16)
    acc = x1
    for j in range(hidden // th):
        gate = _dot(xn, win_ref[:, j * th:(j + 1) * th])
        up = _dot(xn, win_ref[:, hidden + j * th:hidden + (j + 1) * th])
        act = (gate * _sigmoid(gate) * up).astype(BF16)
        acc = acc + _dot(act, wout_ref[j * th:(j + 1) * th, :])
    y_ref[...] = acc


def _out_ffn_kernel(x_ref, o_ref, wmix_ref, gain_ref, win_ref, wout_ref, y_ref, *, th):
    _ffn_tail(x_ref[...], o_ref[...], wmix_ref, gain_ref, win_ref, wout_ref, y_ref, th)


def _out_ffn(x, o, wmix, gain, win, wout, *, tm=1024, th=256):
    n, d = x.shape
    ko = o.shape[1]
    hidden = wout.shape[0]
    return pl.pallas_call(
        functools.partial(_out_ffn_kernel, th=th),
        grid=(n // tm,),
        in_specs=[pl.BlockSpec((tm, d), lambda i: (i, 0)),
                  pl.BlockSpec((tm, ko), lambda i: (i, 0)),
                  _resident((ko, d)),
                  _resident((1, d)),
                  _resident((d, 2 * hidden)),
                  _resident((hidden, d))],
        out_specs=pl.BlockSpec((tm, d), lambda i: (i, 0)),
        out_shape=jax.ShapeDtypeStruct((n, d), F32),
        compiler_params=_cparams(("parallel",)),
        name="out_ffn",
    )(x, o, wmix, gain, win, wout)


def _merge_out_ffn_kernel(x_ref, o0_ref, o1_ref, o2_ref, l0_ref, l1_ref, l2_ref,
                          wmix_ref, gain_ref, win_ref, wout_ref, y_ref, onat_ref, lnat_ref, *, th):
    tm = x_ref.shape[0]
    n_pairs = N_HEADS // 2
    outs, lses = [], []
    for g, ((window, d), o_ref, l_ref) in enumerate(zip(ATT_GROUPS, (o0_ref, o1_ref, o2_ref),
                                                        (l0_ref, l1_ref, l2_ref))):
        if d == 1:
            outs.append([o_ref[0, :, p * LANES:(p + 1) * LANES] for p in range(n_pairs)])
            lses.append(l_ref[0])
            continue
        rows = tm // d
        for r in range(d):
            for p in range(n_pairs):
                onat_ref[g - 1, p, pl.ds(r, rows, stride=d), :] = (
                    o_ref[r, :, p * LANES:(p + 1) * LANES])
            lnat_ref[g - 1, pl.ds(r, rows, stride=d), :] = l_ref[r]
        outs.append([onat_ref[g - 1, p] for p in range(n_pairs)])
        lses.append(lnat_ref[g - 1])
    mx = jnp.maximum(jnp.maximum(lses[0], lses[1]), lses[2])
    wts = [jnp.exp(l - mx) for l in lses]
    inv = 1.0 / (wts[0] + wts[1] + wts[2])
    wts = [w * inv for w in wts]
    left = lax.broadcasted_iota(jnp.int32, (tm, LANES), 1) < ATT_DH
    pairs = []
    for p in range(n_pairs):
        acc = None
        for w, o in zip(wts, outs):
            term = jnp.where(left, w[:, 2 * p:2 * p + 1], w[:, 2 * p + 1:2 * p + 2]) * o[p]
            acc = term if acc is None else acc + term
        pairs.append(acc.astype(BF16))
    mixed16 = jnp.concatenate(pairs, axis=1)
    _ffn_tail(x_ref[...], mixed16, wmix_ref, gain_ref, win_ref, wout_ref, y_ref, th)


def _merge_out_ffn(x, outs, lses, wmix, gain, win, wout, *, batch, seq, tm=512, th=256):
    n, d_model = x.shape
    nt = seq // tm
    hidden = wout.shape[0]
    o_specs = [pl.BlockSpec((None, d, tm // d, ATT_WIDTH), lambda b, i: (b, 0, i, 0))
               for window, d in ATT_GROUPS]
    l_specs = [pl.BlockSpec((None, d, tm // d, LANES), lambda b, i: (b, 0, i, 0))
               for window, d in ATT_GROUPS]
    return pl.pallas_call(
        functools.partial(_merge_out_ffn_kernel, th=th),
        grid=(batch, nt),
        in_specs=[pl.BlockSpec((tm, d_model), lambda b, i: (b * nt + i, 0))] + o_specs + l_specs
                 + [_resident((ATT_WIDTH, d_model)), _resident((1, d_model)),
                    _resident((d_model, 2 * hidden)), _resident((hidden, d_model))],
        out_specs=pl.BlockSpec((tm, d_model), lambda b, i: (b * nt + i, 0)),
        out_shape=jax.ShapeDtypeStruct((n, d_model), F32),
        scratch_shapes=[pltpu.VMEM((len(ATT_GROUPS) - 1, ATT_WIDTH // LANES, tm, LANES), F32),
                        pltpu.VMEM((len(ATT_GROUPS) - 1, tm, LANES), F32)],
        compiler_params=_cparams(("parallel", "parallel")),
        name="merge_out_ffn",
    )(x, *outs, *lses, wmix, gain, win, wout)


def _gdn_layer(xf, gain_mix, w_in, conv_w, a_log, dt_bias, norm_w, *, batch, seq):
    d_model = xf.shape[1]
    main = 4 * N_HEADS * GDN_DK
    wab = jnp.zeros((d_model, LANES), BF16).at[:, :2 * N_HEADS].set(w_in[:, main:].astype(BF16))
    gconst = jnp.zeros((8, LANES), F32)
    gconst = gconst.at[0, :N_HEADS].set(-jnp.exp(a_log.astype(F32)))
    gconst = gconst.at[1, :N_HEADS].set(dt_bias.astype(F32))
    proj, colpack, gct = _gdn_proj(xf, gain_mix, w_in[:, :main].astype(BF16), wab, gconst,
                                   conv_w.astype(F32), batch=batch, seq=seq)
    return _gdn_chunk(proj, colpack, gct, norm_w.reshape(1, GDN_DK).astype(F32),
                      batch=batch, seq=seq)


def _att_layer(xf, gain_mix, w_in, q_norm, k_norm, *, batch, seq):
    n_groups = len(ATT_GROUPS)
    w_in = w_in.astype(BF16)
    ws = [jnp.concatenate([w_in[:, (which * n_groups + gi) * ATT_WIDTH:
                                (which * n_groups + gi + 1) * ATT_WIDTH]
                           for which in range(3)], axis=1) for gi in range(n_groups)]
    projs = _dil_proj(xf, gain_mix, ws, batch=batch, seq=seq)
    qw = jnp.tile(q_norm.astype(F32), N_HEADS).reshape(1, ATT_WIDTH)
    kw = jnp.tile(k_norm.astype(F32), N_HEADS).reshape(1, ATT_WIDTH)
    res = [_att_group(projs[gi], qw, kw, gi=gi, batch=batch, seq=seq) for gi in range(n_groups)]
    return [r[0] for r in res], [r[1] for r in res]


def kernel(x, norm_mix, norm_ffn, gdn_w_in, gdn_conv_w, gdn_a_log, gdn_dt_bias, gdn_norm_w,
           gdn_w_out, dil_w_in, dil_q_norm, dil_k_norm, dil_w_out, ffn_w_in, ffn_w_out):
    batch, seq, d_model = x.shape
    depth = norm_mix.shape[0]
    xf = x.reshape(batch * seq, d_model)
    for i in range(depth):
        j = i // 2
        gain_mix = norm_mix[i].reshape(1, d_model)
        ffn = (norm_ffn[i].reshape(1, d_model), ffn_w_in[i].astype(BF16), ffn_w_out[i].astype(BF16))
        if i % 2 == 0:
            mixed = _gdn_layer(xf, gain_mix, gdn_w_in[j], gdn_conv_w[j], gdn_a_log[j],
                               gdn_dt_bias[j], gdn_norm_w[j], batch=batch, seq=seq)
            xf = _out_ffn(xf, mixed, gdn_w_out[j].astype(BF16), *ffn)
        else:
            outs, lses = _att_layer(xf, gain_mix, dil_w_in[j], dil_q_norm[j], dil_k_norm[j],
                                    batch=batch, seq=seq)
            xf = _merge_out_ffn(xf, outs, lses, dil_w_out[j].astype(BF16), *ffn,
                                batch=batch, seq=seq)
    return xf.reshape(batch, seq, d_model)
```

```python
import functools

import jax
import jax.numpy as jnp
from jax import lax
from jax.experimental import pallas as pl
from jax.experimental.pallas import tpu as pltpu

F32 = jnp.float32
BF16 = jnp.bfloat16

RMS_EPS = 1e-6
L2_EPS = 1e-6

N_HEADS = 8
GDN_DK = 128
GDN_CONV = 4
CHUNK = 64
INV_BASE = 8

ATT_GROUPS = ((128, 1), (512, 4), (2048, 16))
ATT_SPAN = 128
ATT_DH = 64
ATT_WIDTH = N_HEADS * ATT_DH
ALIBI_MAX_BIAS = 8.0
LOG2_E = 1.4426950408889634
LN_2 = 0.6931471805599453
ONES_ROWS = 16

LANES = 128
VMEM_LIMIT = 56 * 1024 * 1024


def _cparams(sem):
    return pltpu.CompilerParams(dimension_semantics=sem, vmem_limit_bytes=VMEM_LIMIT)


def _resident(shape):
    nd = len(shape)
    return pl.BlockSpec(shape, lambda *_: (0,) * nd, pipeline_mode=pl.Buffered(1))


def _rms_scale(x, gain):
    ms = jnp.mean(x * x, axis=-1, keepdims=True)
    return x * lax.rsqrt(ms + RMS_EPS) * gain


def _sigmoid(x):
    return 1.0 / (1.0 + jnp.exp(-x))


def _dot(a, b):
    return jnp.dot(a, b, preferred_element_type=F32)


def _dot_nt(a, b):
    return lax.dot_general(a, b, (((1,), (1,)), ((), ())), preferred_element_type=F32)


def _dot_tn(a, b):
    return lax.dot_general(a, b, (((0,), (0,)), ((), ())), preferred_element_type=F32)


def _dil_proj_kernel(x_ref, gain_ref, w0_ref, w1_ref, w2_ref, p0_ref, p1_ref, p2_ref,
                     xn_ref, perm_ref):
    tm = x_ref.shape[0]
    xn = _rms_scale(x_ref[...], gain_ref[...])
    n_tiles = xn.shape[1] // LANES
    for c in range(n_tiles):
        xn_ref[c] = xn[:, c * LANES:(c + 1) * LANES]
    for (window, d), w_ref, p_ref in zip(ATT_GROUPS, (w0_ref, w1_ref, w2_ref),
                                         (p0_ref, p1_ref, p2_ref)):
        rows = tm // d
        if d == 1:
            lhs = xn.astype(BF16)
        else:
            for r in range(d):
                for c in range(n_tiles):
                    perm_ref[r * rows:(r + 1) * rows, c * LANES:(c + 1) * LANES] = (
                        xn_ref[c, pl.ds(r, rows, stride=d), :].astype(BF16))
            lhs = perm_ref[...]
        for j in range(3):
            cols = slice(j * ATT_WIDTH, (j + 1) * ATT_WIDTH)
            res = _dot(lhs, w_ref[:, cols]).astype(BF16)
            for r in range(d):
                p_ref[r, :, cols] = res[r * rows:(r + 1) * rows]


def _dil_proj(x, gain, ws, *, batch, seq, tm=512):
    n, d_model = x.shape
    nt = seq // tm
    out_specs, out_shape = [], []
    for window, d in ATT_GROUPS:
        out_specs.append(pl.BlockSpec((None, d, tm // d, 3 * ATT_WIDTH), lambda b, i: (b, 0, i, 0)))
        out_shape.append(jax.ShapeDtypeStruct((batch, d, seq // d, 3 * ATT_WIDTH), BF16))
    return pl.pallas_call(
        _dil_proj_kernel,
        grid=(batch, nt),
        in_specs=[pl.BlockSpec((tm, d_model), lambda b, i: (b * nt + i, 0)),
                  _resident((1, d_model))] + [_resident((d_model, 3 * ATT_WIDTH))] * 3,
        out_specs=out_specs,
        out_shape=out_shape,
        scratch_shapes=[pltpu.VMEM((d_model // LANES, tm, LANES), F32),
                        pltpu.VMEM((tm, d_model), BF16)],
        compiler_params=_cparams(("parallel", "parallel")),
        name="dil_proj",
    )(x, gain, *ws)


def _gdn_proj_kernel(x_ref, gain_ref, w_ref, wab_ref, gconst_ref, convw_ref,
                     proj_ref, colpack_ref, gct_ref, halo_ref, *, tn):
    tm = x_ref.shape[0]
    width = N_HEADS * GDN_DK

    @pl.when(pl.program_id(1) == 0)
    def _():
        halo_ref[...] = jnp.zeros_like(halo_ref)

    xn = _rms_scale(x_ref[...], gain_ref[...]).astype(BF16)
    for j in range(w_ref.shape[1] // tn):
        cols = slice(j * tn, (j + 1) * tn)
        res = _dot(xn, w_ref[:, cols])
        if j * tn >= 3 * width:
            proj_ref[:, cols] = res.astype(BF16)
            continue
        hist = jnp.concatenate([halo_ref[:, cols], res], axis=0)
        halo_ref[:, cols] = res[tm - 8:, :]
        y = convw_ref[GDN_CONV - 1:GDN_CONV, cols] * res
        for tap in range(GDN_CONV - 1):
            off = 8 - (GDN_CONV - 1) + tap
            y = y + convw_ref[tap:tap + 1, cols] * hist[off:off + tm, :]
        y = y * _sigmoid(y)
        for h in range(tn // GDN_DK):
            hc = slice(h * GDN_DK, (h + 1) * GDN_DK)
            yh = y[:, hc]
            if j * tn < 2 * width:
                inv = lax.rsqrt(jnp.sum(yh * yh, axis=-1, keepdims=True) + L2_EPS)
                yh = yh * (inv * (GDN_DK ** -0.5) if j * tn < width else inv)
            proj_ref[:, j * tn + h * GDN_DK:j * tn + (h + 1) * GDN_DK] = yh.astype(BF16)

    ab = _dot(xn, wab_ref[...])
    lane = lax.broadcasted_iota(jnp.int32, (tm, LANES), 1)
    row = lax.broadcasted_iota(jnp.int32, (tm, LANES), 0)
    head_lanes = lane < N_HEADS
    neg_a = gconst_ref[0:1, :]
    dt_bias = gconst_ref[1:2, :]
    sp_in = ab + dt_bias
    softplus = jnp.maximum(sp_in, 0.0) + jnp.log(1.0 + jnp.exp(-jnp.abs(sp_in)))
    g = neg_a * softplus
    beta = _sigmoid(ab)

    rc = row & (CHUNK - 1)
    gc = g
    s = 1
    while s < CHUNK:
        gc = gc + jnp.where(rc >= s, pltpu.roll(gc, s, axis=0), 0.0)
        s *= 2
    gc3 = gc.reshape(tm // CHUNK, CHUNK, LANES)
    gl = jnp.broadcast_to(gc3[:, CHUNK - 1:CHUNK, :], gc3.shape).reshape(tm, LANES)

    eg = jnp.exp(gc)
    beta0 = pltpu.roll(beta, LANES - N_HEADS, axis=1)

    def put(v, at):
        v = jnp.where(head_lanes, v, 0.0)
        return pltpu.roll(v, at, axis=1) if at else v

    colpack_ref[...] = (put(gc, 0) + jnp.where((lane >= 8) & (lane < 16), beta, 0.0)
                        + put(eg, 16) + put(jnp.exp(gl - gc), 24) + put(jnp.exp(gl), 32)
                        + put(beta0 * eg, 40))

    sel = (lax.broadcasted_iota(jnp.int32, (N_HEADS, LANES), 0)
           == lax.broadcasted_iota(jnp.int32, (N_HEADS, LANES), 1)).astype(BF16)
    gcm = jnp.where(head_lanes, gc, 0.0)
    hi = gcm.astype(BF16)
    r1 = gcm - hi.astype(F32)
    mid = r1.astype(BF16)
    lo = (r1 - mid.astype(F32)).astype(BF16)
    for c in range(tm // CHUNK):
        sl = slice(c * CHUNK, (c + 1) * CHUNK)
        gct_ref[c] = (_dot_nt(sel, hi[sl]) + _dot_nt(sel, mid[sl])) + _dot_nt(sel, lo[sl])


def _gdn_proj(x, gain, w, wab, gconst, conv_w, *, batch, seq, tm=512, tn=512):
    n, d = x.shape
    wout = w.shape[1]
    nt = seq // tm
    qkv = conv_w.shape[1]

    def rows(b, i):
        return (b * nt + i, 0)

    return pl.pallas_call(
        functools.partial(_gdn_proj_kernel, tn=tn),
        grid=(batch, nt),
        in_specs=[pl.BlockSpec((tm, d), rows),
                  _resident((1, d)),
                  _resident((d, wout)),
                  _resident((d, LANES)),
                  _resident((8, LANES)),
                  _resident((GDN_CONV, qkv))],
        out_specs=[pl.BlockSpec((tm, wout), rows),
                   pl.BlockSpec((tm, LANES), rows),
                   pl.BlockSpec((tm // CHUNK, N_HEADS, CHUNK), lambda b, i: (b * nt + i, 0, 0))],
        out_shape=[jax.ShapeDtypeStruct((n, wout), BF16),
                   jax.ShapeDtypeStruct((n, LANES), F32),
                   jax.ShapeDtypeStruct((n // CHUNK, N_HEADS, CHUNK), F32)],
        scratch_shapes=[pltpu.VMEM((8, qkv), F32)],
        compiler_params=_cparams(("parallel", "arbitrary")),
        name="gdn_proj",
    )(x, gain, w, wab, gconst, conv_w)


def _unit_lower_inverse(nms, masks):
    eye, m_base, level_masks = masks
    nbs = [jnp.where(m_base, nm, 0.0) for nm in nms]
    xs = [eye - nb for nb in nbs]
    nb16 = [nb.astype(BF16) for nb in nbs]
    ps = [_dot(a, a) for a in nb16]
    s = 2
    while s < INV_BASE:
        p16 = [p.astype(BF16) for p in ps]
        s *= 2
        if s < INV_BASE:
            both = [_dot(a, jnp.concatenate([a, x.astype(BF16)], axis=1)) for a, x in zip(p16, xs)]
            ps = [b[:, :CHUNK] for b in both]
            xs = [x + b[:, CHUNK:] for x, b in zip(xs, both)]
        else:
            xs = [x + _dot(a, x.astype(BF16)) for a, x in zip(p16, xs)]
    for m_off in level_masks:
        x16 = [x.astype(BF16) for x in xs]
        ws = [_dot(jnp.where(m_off, nm, 0.0).astype(BF16), a) for nm, a in zip(nms, x16)]
        xs = [x - _dot(a, w.astype(BF16)) for x, a, w in zip(xs, x16, ws)]
    return xs


def _gdn_chunk_kernel(q_ref, k_ref, v_ref, z_ref, colpack_ref, gct_ref, normw_ref,
                      o_ref, state_ref, u_ref, w_ref, qd_ref, kt_ref, qk_ref, *, prep_chunks):
    t = pl.program_id(1)
    tt = q_ref.shape[0]
    n_chunks = tt // CHUNK

    @pl.when(t == 0)
    def _():
        state_ref[...] = jnp.zeros_like(state_ref)

    ri = lax.broadcasted_iota(jnp.int32, (CHUNK, CHUNK), 0)
    ci = lax.broadcasted_iota(jnp.int32, (CHUNK, CHUNK), 1)
    causal = ri >= ci
    strict = ri > ci
    eye = (ri == ci).astype(F32)

    def same_block(sz):
        shift = sz.bit_length() - 1
        return (ri >> shift) == (ci >> shift)

    level_masks = []
    sz = INV_BASE
    while sz < CHUNK:
        level_masks.append(same_block(2 * sz) & jnp.logical_not(same_block(sz)))
        sz *= 2
    masks = (eye, same_block(INV_BASE), tuple(level_masks))
    normw = normw_ref[...]

    def hcols(h):
        return slice(h * GDN_DK, (h + 1) * GDN_DK)

    def crows(c):
        return slice(c * CHUNK, (c + 1) * CHUNK)

    for c0 in range(0, n_chunks, prep_chunks):
        pairs = [(c, h) for c in range(c0, c0 + prep_chunks) for h in range(N_HEADS)]
        cps = {c: colpack_ref[crows(c), :] for c in range(c0, c0 + prep_chunks)}
        gts = {c: gct_ref[c] for c in range(c0, c0 + prep_chunks)}
        q16 = [q_ref[crows(c), hcols(h)] for c, h in pairs]
        k16 = [k_ref[crows(c), hcols(h)] for c, h in pairs]
        qk_kk = [_dot_nt(jnp.concatenate([qb, kb], axis=0), kb)
                 for qb, kb in zip(q16, k16)]
        decay = [jnp.exp(jnp.where(causal, cps[c][:, h:h + 1] - gts[c][h:h + 1, :], -jnp.inf))
                 for c, h in pairs]
        for (c, h), a, d in zip(pairs, qk_kk, decay):
            qk_ref[crows(c), h * CHUNK:(h + 1) * CHUNK] = (a[:CHUNK] * d).astype(BF16)
        nms = [jnp.where(strict, a[CHUNK:] * cps[c][:, 8 + h:9 + h] * d, 0.0)
               for (c, h), a, d in zip(pairs, qk_kk, decay)]
        tinv = _unit_lower_inverse(nms, masks)
        qs = [qb.astype(F32) for qb in q16]
        ks = [kb.astype(F32) for kb in k16]
        rhs = [jnp.concatenate([v_ref[crows(c), hcols(h)].astype(F32) * cps[c][:, 8 + h:9 + h],
                                k * cps[c][:, 40 + h:41 + h]], axis=1).astype(BF16)
               for (c, h), k in zip(pairs, ks)]
        sol = [_dot(ti.astype(BF16), r) for ti, r in zip(tinv, rhs)]
        for (c, h), s_, q, k in zip(pairs, sol, qs, ks):
            u_ref[crows(c), hcols(h)] = s_[:, :GDN_DK]
            w_ref[crows(c), hcols(h)] = s_[:, GDN_DK:].astype(BF16)
            qd_ref[crows(c), hcols(h)] = (q * cps[c][:, 16 + h:17 + h]).astype(BF16)
            kt_ref[crows(c), hcols(h)] = (k * cps[c][:, 24 + h:25 + h]).astype(BF16)

    heads = range(N_HEADS)
    states = [state_ref[h] for h in heads]
    for c in range(n_chunks):
        rows = crows(c)
        s16 = [s_.astype(BF16) for s_ in states]
        ws_qs = [_dot(jnp.concatenate([w_ref[rows, hcols(h)], qd_ref[rows, hcols(h)]], axis=0),
                      s16[h]) for h in heads]
        vn16 = [(u_ref[rows, hcols(h)] - ws_qs[h][:CHUNK]).astype(BF16) for h in heads]
        outs = [ws_qs[h][CHUNK:] + _dot(qk_ref[rows, h * CHUNK:(h + 1) * CHUNK], vn16[h])
                for h in heads]
        elast = colpack_ref[c * CHUNK:c * CHUNK + 1, :]
        states = [states[h] * elast[:, 32 + h:33 + h] + _dot_tn(kt_ref[rows, hcols(h)], vn16[h])
                  for h in heads]
        for h in heads:
            o = outs[h]
            o = o * lax.rsqrt(jnp.mean(o * o, axis=-1, keepdims=True) + RMS_EPS)
            z = z_ref[rows, hcols(h)].astype(F32)
            o_ref[rows, hcols(h)] = (o * normw * (z * _sigmoid(z))).astype(o_ref.dtype)
    for h in heads:
        state_ref[h] = states[h]


def _gdn_chunk(proj, colpack, gct, norm_w, *, batch, seq, tt=256, prep_chunks=4):
    n = batch * seq
    nt = seq // tt
    width = N_HEADS * GDN_DK

    def col_block(cb):
        return pl.BlockSpec((tt, width), lambda b, t: (b * nt + t, cb))

    return pl.pallas_call(
        functools.partial(_gdn_chunk_kernel, prep_chunks=prep_chunks),
        grid=(batch, nt),
        in_specs=[col_block(0), col_block(1), col_block(2), col_block(3),
                  pl.BlockSpec((tt, LANES), lambda b, t: (b * nt + t, 0)),
                  pl.BlockSpec((tt // CHUNK, N_HEADS, CHUNK), lambda b, t: (b * nt + t, 0, 0)),
                  _resident((1, GDN_DK))],
        out_specs=pl.BlockSpec((tt, width), lambda b, t: (b * nt + t, 0)),
        out_shape=jax.ShapeDtypeStruct((n, width), BF16),
        scratch_shapes=[pltpu.VMEM((N_HEADS, GDN_DK, GDN_DK), F32),
                        pltpu.VMEM((tt, width), F32),
                        pltpu.VMEM((tt, width), BF16),
                        pltpu.VMEM((tt, width), BF16),
                        pltpu.VMEM((tt, width), BF16),
                        pltpu.VMEM((tt, N_HEADS * CHUNK), BF16)],
        compiler_params=_cparams(("parallel", "arbitrary")),
        name="gdn_chunk",
    )(proj, proj, proj, proj, colpack, gct, norm_w)


def _att_group_kernel(q_ref, k_ref, v_ref, qw_ref, kw_ref, o_ref, lse_ref, kn_ref, vt_ref,
                      bias_ref, *, dilation, slopes, nq):
    n = pl.program_id(2)
    span = ATT_SPAN
    rows_per_step = nq * span
    n_pairs = N_HEADS // 2

    @pl.when(n == 0)
    def _():
        kn_ref[0:span, :] = jnp.zeros((span, ATT_WIDTH), BF16)
        vt_ref[:, :, 0:span] = jnp.zeros((n_pairs, LANES + ONES_ROWS, span), BF16)

    bi = lax.broadcasted_iota(jnp.int32, (LANES, LANES), 0) // ATT_DH
    bj = lax.broadcasted_iota(jnp.int32, (LANES, LANES), 1) // ATT_DH
    half_ones = (bi == bj).astype(BF16)

    def head_norm(x16, w):
        x = x16.astype(F32)
        ms = _dot((x * x).astype(BF16), half_ones) * (1.0 / ATT_DH)
        return (x * lax.rsqrt(ms + RMS_EPS) * w).astype(BF16)

    q_gain = ATT_DH ** -0.5 * LOG2_E
    qn = []
    for p in range(n_pairs):
        cols = slice(p * LANES, (p + 1) * LANES)
        kn_ref[span:, cols] = head_norm(k_ref[:, cols], kw_ref[:, cols])
        qn.append(head_norm(q_ref[:, cols], qw_ref[:, cols] * q_gain))
        vt_ref[p, :LANES, span:] = v_ref[:, cols].astype(F32).T.astype(BF16)
        vt_ref[p, LANES:, span:] = jnp.ones((ONES_ROWS, rows_per_step), BF16)

    kj = lax.broadcasted_iota(jnp.int32, (2 * span, 2 * span), 0)
    ql = lax.broadcasted_iota(jnp.int32, (2 * span, 2 * span), 1)
    steps = (ql & (span - 1)) + span - kj
    band = (steps >= 0) & (steps <= span)
    dist = (steps * dilation).astype(F32)
    for p in range(n_pairs):
        slope = jnp.where(ql < span, slopes[2 * p], slopes[2 * p + 1])
        bias_ref[p] = jnp.where(band, -(slope * LOG2_E) * dist, -jnp.inf)

    lane_s = lax.broadcasted_iota(jnp.int32, (span, LANES), 1)
    left_s = lane_s < ATT_DH
    row_s = lax.broadcasted_iota(jnp.int32, (span, LANES), 0)

    first_key = jnp.where(n == 0, span, 0)
    pairs = range(n_pairs)
    for i in range(nq):
        qrows = slice(i * span, (i + 1) * span)
        krows = slice(i * span, (i + 2) * span)
        q_both = []
        for p in pairs:
            q_pair = qn[p][qrows]
            zero = jnp.zeros_like(q_pair)
            q_both.append(jnp.concatenate([jnp.where(left_s, q_pair, zero),
                                           jnp.where(left_s, zero, q_pair)], axis=0))
        ss = [_dot_nt(kn_ref[krows, p * LANES:(p + 1) * LANES], q_both[p]) + bias_ref[p]
              for p in pairs]
        if i == 0:
            ss = [jnp.where(kj >= first_key, s, -jnp.inf) for s in ss]
        ms = [jnp.max(s, axis=0, keepdims=True) for s in ss]
        prs = [jnp.exp2(s - m).astype(BF16) for s, m in zip(ss, ms)]
        accs = [_dot(vt_ref[p, :, krows], prs[p]) for p in pairs]
        lse_t = jnp.zeros((span, LANES), F32)
        for p in pairs:
            l = accs[p][LANES:LANES + 1, :]
            inv = 1.0 / l
            o_t = jnp.concatenate([accs[p][:ATT_DH, :span] * inv[:, :span],
                                   accs[p][ATT_DH:LANES, span:] * inv[:, span:]], axis=0)
            o_ref[qrows, p * LANES:(p + 1) * LANES] = o_t.T
            lse = ms[p] * LN_2 + jnp.log(l)
            lse_t = jnp.where(row_s == 2 * p, lse[:, :span], lse_t)
            lse_t = jnp.where(row_s == 2 * p + 1, lse[:, span:], lse_t)
        lse_ref[qrows, :] = lse_t.T

    kn_ref[0:span, :] = kn_ref[rows_per_step:, :]
    vt_ref[:, :, 0:span] = vt_ref[:, :, rows_per_step:]


def _att_group(proj_g, qw, kw, *, gi, batch, seq):
    window, d = ATT_GROUPS[gi]
    assert window // d == ATT_SPAN
    n_groups = len(ATT_GROUPS)
    sub = seq // d
    nblk = sub // ATT_SPAN
    nq = min(4, nblk)
    rows = nq * ATT_SPAN
    slopes = tuple(2.0 ** (-ALIBI_MAX_BIAS * (gi * N_HEADS + h + 1) / (n_groups * N_HEADS))
                   for h in range(N_HEADS))

    def proj_block(which):
        return pl.BlockSpec((None, None, rows, ATT_WIDTH), lambda b, r, n: (b, r, n, which))

    return pl.pallas_call(
        functools.partial(_att_group_kernel, dilation=d, slopes=slopes, nq=nq),
        grid=(batch, d, nblk // nq),
        in_specs=[proj_block(0), proj_block(1), proj_block(2),
                  _resident((1, ATT_WIDTH)), _resident((1, ATT_WIDTH))],
        out_specs=[pl.BlockSpec((None, None, rows, ATT_WIDTH), lambda b, r, n: (b, r, n, 0)),
                   pl.BlockSpec((None, None, rows, LANES), lambda b, r, n: (b, r, n, 0))],
        out_shape=[jax.ShapeDtypeStruct((batch, d, sub, ATT_WIDTH), F32),
                   jax.ShapeDtypeStruct((batch, d, sub, LANES), F32)],
        scratch_shapes=[pltpu.VMEM((ATT_SPAN + rows, ATT_WIDTH), BF16),
                        pltpu.VMEM((N_HEADS // 2, LANES + ONES_ROWS, ATT_SPAN + rows), BF16),
                        pltpu.VMEM((N_HEADS // 2, 2 * ATT_SPAN, 2 * ATT_SPAN), F32)],
        compiler_params=_cparams(("parallel", "parallel", "arbitrary")),
        name=f"att_group{gi}",
    )(proj_g, proj_g, proj_g, qw, kw)


def _ffn_tail(x, mixed16, wmix_ref, gain_ref, win_ref, wout_ref, y_ref, th):
    hidden = wout_ref.shape[0]
    x1 = x + _dot(mixed16, wmix_ref[...])
    xn = _rms_scale(x1, gain_ref[...]).astype(BF16)
    acc = x1
    for j in range(hidden // th):
        gate = _dot(xn, win_ref[:, j * th:(j + 1) * th])
        up = _dot(xn, win_ref[:, hidden + j * th:hidden + (j + 1) * th])
        act = (gate * _sigmoid(gate) * up).astype(BF16)
        acc = acc + _dot(act, wout_ref[j * th:(j + 1) * th, :])
    y_ref[...] = acc


def _out_ffn_kernel(x_ref, o_ref, wmix_ref, gain_ref, win_ref, wout_ref, y_ref, *, th):
    _ffn_tail(x_ref[...], o_ref[...], wmix_ref, gain_ref, win_ref, wout_ref, y_ref, th)


def _out_ffn(x, o, wmix, gain, win, wout, *, tm=1024, th=256):
    n, d = x.shape
    ko = o.shape[1]
    hidden = wout.shape[0]
    return pl.pallas_call(
        functools.partial(_out_ffn_kernel, th=th),
        grid=(n // tm,),
        in_specs=[pl.BlockSpec((tm, d), lambda i: (i, 0)),
                  pl.BlockSpec((tm, ko), lambda i: (i, 0)),
                  _resident((ko, d)),
                  _resident((1, d)),
                  _resident((d, 2 * hidden)),
                  _resident((hidden, d))],
        out_specs=pl.BlockSpec((tm, d), lambda i: (i, 0)),
        out_shape=jax.ShapeDtypeStruct((n, d), F32),
        compiler_params=_cparams(("parallel",)),
        name="out_ffn",
    )(x, o, wmix, gain, win, wout)


def _merge_out_ffn_kernel(x_ref, o0_ref, o1_ref, o2_ref, l0_ref, l1_ref, l2_ref,
                          wmix_ref, gain_ref, win_ref, wout_ref, y_ref, onat_ref, lnat_ref, *, th):
    tm = x_ref.shape[0]
    n_pairs = N_HEADS // 2
    outs, lses = [], []
    for g, ((window, d), o_ref, l_ref) in enumerate(zip(ATT_GROUPS, (o0_ref, o1_ref, o2_ref),
                                                        (l0_ref, l1_ref, l2_ref))):
        if d == 1:
            outs.append([o_ref[0, :, p * LANES:(p + 1) * LANES] for p in range(n_pairs)])
            lses.append(l_ref[0])
            continue
        rows = tm // d
        for r in range(d):
            for p in range(n_pairs):
                onat_ref[g - 1, p, pl.ds(r, rows, stride=d), :] = (
                    o_ref[r, :, p * LANES:(p + 1) * LANES])
            lnat_ref[g - 1, pl.ds(r, rows, stride=d), :] = l_ref[r]
        outs.append([onat_ref[g - 1, p] for p in range(n_pairs)])
        lses.append(lnat_ref[g - 1])
    mx = jnp.maximum(jnp.maximum(lses[0], lses[1]), lses[2])
    wts = [jnp.exp(l - mx) for l in lses]
    inv = 1.0 / (wts[0] + wts[1] + wts[2])
    wts = [w * inv for w in wts]
    left = lax.broadcasted_iota(jnp.int32, (tm, LANES), 1) < ATT_DH
    pairs = []
    for p in range(n_pairs):
        acc = None
        for w, o in zip(wts, outs):
            term = jnp.where(left, w[:, 2 * p:2 * p + 1], w[:, 2 * p + 1:2 * p + 2]) * o[p]
            acc = term if acc is None else acc + term
        pairs.append(acc.astype(BF16))
    mixed16 = jnp.concatenate(pairs, axis=1)
    _ffn_tail(x_ref[...], mixed16, wmix_ref, gain_ref, win_ref, wout_ref, y_ref, th)


def _merge_out_ffn(x, outs, lses, wmix, gain, win, wout, *, batch, seq, tm=512, th=256):
    n, d_model = x.shape
    nt = seq // tm
    hidden = wout.shape[0]
    o_specs = [pl.BlockSpec((None, d, tm // d, ATT_WIDTH), lambda b, i: (b, 0, i, 0))
               for window, d in ATT_GROUPS]
    l_specs = [pl.BlockSpec((None, d, tm // d, LANES), lambda b, i: (b, 0, i, 0))
               for window, d in ATT_GROUPS]
    return pl.pallas_call(
        functools.partial(_merge_out_ffn_kernel, th=th),
        grid=(batch, nt),
        in_specs=[pl.BlockSpec((tm, d_model), lambda b, i: (b * nt + i, 0))] + o_specs + l_specs
                 + [_resident((ATT_WIDTH, d_model)), _resident((1, d_model)),
                    _resident((d_model, 2 * hidden)), _resident((hidden, d_model))],
        out_specs=pl.BlockSpec((tm, d_model), lambda b, i: (b * nt + i, 0)),
        out_shape=jax.ShapeDtypeStruct((n, d_model), F32),
        scratch_shapes=[pltpu.VMEM((len(ATT_GROUPS) - 1, ATT_WIDTH // LANES, tm, LANES), F32),
                        pltpu.VMEM((len(ATT_GROUPS) - 1, tm, LANES), F32)],
        compiler_params=_cparams(("parallel", "parallel")),
        name="merge_out_ffn",
    )(x, *outs, *lses, wmix, gain, win, wout)


def _gdn_layer(xf, gain_mix, w_in, conv_w, a_log, dt_bias, norm_w, *, batch, seq):
    d_model = xf.shape[1]
    main = 4 * N_HEADS * GDN_DK
    wab = jnp.zeros((d_model, LANES), BF16).at[:, :2 * N_HEADS].set(w_in[:, main:].astype(BF16))
    gconst = jnp.zeros((8, LANES), F32)
    gconst = gconst.at[0, :N_HEADS].set(-jnp.exp(a_log.astype(F32)))
    gconst = gconst.at[1, :N_HEADS].set(dt_bias.astype(F32))
    proj, colpack, gct = _gdn_proj(xf, gain_mix, w_in[:, :main].astype(BF16), wab, gconst,
                                   conv_w.astype(F32), batch=batch, seq=seq)
    return _gdn_chunk(proj, colpack, gct, norm_w.reshape(1, GDN_DK).astype(F32),
                      batch=batch, seq=seq)


def _att_layer(xf, gain_mix, w_in, q_norm, k_norm, *, batch, seq):
    n_groups = len(ATT_GROUPS)
    w_in = w_in.astype(BF16)
    ws = [jnp.concatenate([w_in[:, (which * n_groups + gi) * ATT_WIDTH:
                                (which * n_groups + gi + 1) * ATT_WIDTH]
                           for which in range(3)], axis=1) for gi in range(n_groups)]
    projs = _dil_proj(xf, gain_mix, ws, batch=batch, seq=seq)
    qw = jnp.tile(q_norm.astype(F32), N_HEADS).reshape(1, ATT_WIDTH)
    kw = jnp.tile(k_norm.astype(F32), N_HEADS).reshape(1, ATT_WIDTH)
    res = [_att_group(projs[gi], qw, kw, gi=gi, batch=batch, seq=seq) for gi in range(n_groups)]
    return [r[0] for r in res], [r[1] for r in res]


def kernel(x, norm_mix, norm_ffn, gdn_w_in, gdn_conv_w, gdn_a_log, gdn_dt_bias, gdn_norm_w,
           gdn_w_out, dil_w_in, dil_q_norm, dil_k_norm, dil_w_out, ffn_w_in, ffn_w_out):
    batch, seq, d_model = x.shape
    depth = norm_mix.shape[0]
    xf = x.reshape(batch * seq, d_model)
    for i in range(depth):
        j = i // 2
        gain_mix = norm_mix[i].reshape(1, d_model)
        ffn = (norm_ffn[i].reshape(1, d_model), ffn_w_in[i].astype(BF16), ffn_w_out[i].astype(BF16))
        if i % 2 == 0:
            mixed = _gdn_layer(xf, gain_mix, gdn_w_in[j], gdn_conv_w[j], gdn_a_log[j],
                               gdn_dt_bias[j], gdn_norm_w[j], batch=batch, seq=seq)
            xf = _out_ffn(xf, mixed, gdn_w_out[j].astype(BF16), *ffn)
        else:
            outs, lses = _att_layer(xf, gain_mix, dil_w_in[j], dil_q_norm[j], dil_k_norm[j],
                                    batch=batch, seq=seq)
            xf = _merge_out_ffn(xf, outs, lses, dil_w_out[j].astype(BF16), *ffn,
                                batch=batch, seq=seq)
    return xf.reshape(batch, seq, d_model)
```

```python
import functools

import jax
import jax.numpy as jnp
from jax import lax
from jax.experimental import pallas as pl
from jax.experimental.pallas import tpu as pltpu

F32 = jnp.float32
BF16 = jnp.bfloat16

RMS_EPS = 1e-6
L2_EPS = 1e-6

N_HEADS = 8
GDN_DK = 128
GDN_CONV = 4
CHUNK = 64
INV_BASE = 8

ATT_GROUPS = ((128, 1), (512, 4), (2048, 16))
ATT_SPAN = 128
ATT_DH = 64
ATT_WIDTH = N_HEADS * ATT_DH
ALIBI_MAX_BIAS = 8.0
LOG2_E = 1.4426950408889634
LN_2 = 0.6931471805599453
ONES_ROWS = 16

LANES = 128
VMEM_LIMIT = 56 * 1024 * 1024


def _cparams(sem):
    return pltpu.CompilerParams(dimension_semantics=sem, vmem_limit_bytes=VMEM_LIMIT)


def _resident(shape):
    nd = len(shape)
    return pl.BlockSpec(shape, lambda *_: (0,) * nd, pipeline_mode=pl.Buffered(1))


def _rms_scale(x, gain):
    ms = jnp.mean(x * x, axis=-1, keepdims=True)
    return x * lax.rsqrt(ms + RMS_EPS) * gain


def _sigmoid(x):
    return 1.0 / (1.0 + jnp.exp(-x))


def _dot(a, b):
    return jnp.dot(a, b, preferred_element_type=F32)


def _dot_nt(a, b):
    return lax.dot_general(a, b, (((1,), (1,)), ((), ())), preferred_element_type=F32)


def _dot_tn(a, b):
    return lax.dot_general(a, b, (((0,), (0,)), ((), ())), preferred_element_type=F32)


def _dil_proj_kernel(x_ref, gain_ref, w0_ref, w1_ref, w2_ref, *refs):
    n_out = 3 * len(ATT_GROUPS)
    out_refs, (xn_ref, perm_ref) = refs[:n_out], refs[n_out:]
    tm = x_ref.shape[0]
    xn = _rms_scale(x_ref[...], gain_ref[...])
    n_tiles = xn.shape[1] // LANES
    for c in range(n_tiles):
        xn_ref[c] = xn[:, c * LANES:(c + 1) * LANES]
    for g, ((window, d), w_ref) in enumerate(zip(ATT_GROUPS, (w0_ref, w1_ref, w2_ref))):
        rows = tm // d
        if d == 1:
            lhs = xn.astype(BF16)
        else:
            for r in range(d):
                for c in range(n_tiles):
                    perm_ref[r * rows:(r + 1) * rows, c * LANES:(c + 1) * LANES] = (
                        xn_ref[c, pl.ds(r, rows, stride=d), :].astype(BF16))
            lhs = perm_ref[...]
        for j in range(3):
            cols = slice(j * ATT_WIDTH, (j + 1) * ATT_WIDTH)
            res = _dot(lhs, w_ref[:, cols]).astype(BF16)
            for r in range(d):
                out_refs[3 * g + j][r] = res[r * rows:(r + 1) * rows]


def _dil_proj(x, gain, ws, *, batch, seq, tm=512):
    n, d_model = x.shape
    nt = seq // tm
    out_specs, out_shape = [], []
    for window, d in ATT_GROUPS:
        for _ in range(3):
            out_specs.append(pl.BlockSpec((None, d, tm // d, ATT_WIDTH), lambda b, i: (b, 0, i, 0)))
            out_shape.append(jax.ShapeDtypeStruct((batch, d, seq // d, ATT_WIDTH), BF16))
    return pl.pallas_call(
        _dil_proj_kernel,
        grid=(batch, nt),
        in_specs=[pl.BlockSpec((tm, d_model), lambda b, i: (b * nt + i, 0)),
                  _resident((1, d_model))] + [_resident((d_model, 3 * ATT_WIDTH))] * 3,
        out_specs=out_specs,
        out_shape=out_shape,
        scratch_shapes=[pltpu.VMEM((d_model // LANES, tm, LANES), F32),
                        pltpu.VMEM((tm, d_model), BF16)],
        compiler_params=_cparams(("parallel", "parallel")),
        name="dil_proj",
    )(x, gain, *ws)


def _gdn_proj_kernel(x_ref, gain_ref, w_ref, wab_ref, gconst_ref, convw_ref,
                     proj_ref, colpack_ref, gct_ref, halo_ref, *, tn):
    tm = x_ref.shape[0]
    width = N_HEADS * GDN_DK

    @pl.when(pl.program_id(1) == 0)
    def _():
        halo_ref[...] = jnp.zeros_like(halo_ref)

    xn = _rms_scale(x_ref[...], gain_ref[...]).astype(BF16)
    for j in range(w_ref.shape[1] // tn):
        cols = slice(j * tn, (j + 1) * tn)
        res = _dot(xn, w_ref[:, cols])
        if j * tn >= 3 * width:
            proj_ref[:, cols] = res.astype(BF16)
            continue
        hist = jnp.concatenate([halo_ref[:, cols], res], axis=0)
        halo_ref[:, cols] = res[tm - 8:, :]
        y = convw_ref[GDN_CONV - 1:GDN_CONV, cols] * res
        for tap in range(GDN_CONV - 1):
            off = 8 - (GDN_CONV - 1) + tap
            y = y + convw_ref[tap:tap + 1, cols] * hist[off:off + tm, :]
        y = y * _sigmoid(y)
        for h in range(tn // GDN_DK):
            hc = slice(h * GDN_DK, (h + 1) * GDN_DK)
            yh = y[:, hc]
            if j * tn < 2 * width:
                inv = lax.rsqrt(jnp.sum(yh * yh, axis=-1, keepdims=True) + L2_EPS)
                yh = yh * (inv * (GDN_DK ** -0.5) if j * tn < width else inv)
            proj_ref[:, j * tn + h * GDN_DK:j * tn + (h + 1) * GDN_DK] = yh.astype(BF16)

    ab = _dot(xn, wab_ref[...])
    lane = lax.broadcasted_iota(jnp.int32, (tm, LANES), 1)
    row = lax.broadcasted_iota(jnp.int32, (tm, LANES), 0)
    head_lanes = lane < N_HEADS
    neg_a = gconst_ref[0:1, :]
    dt_bias = gconst_ref[1:2, :]
    sp_in = ab + dt_bias
    softplus = jnp.maximum(sp_in, 0.0) + jnp.log(1.0 + jnp.exp(-jnp.abs(sp_in)))
    g = neg_a * softplus
    beta = _sigmoid(ab)

    rc = row & (CHUNK - 1)
    gc = g
    s = 1
    while s < CHUNK:
        gc = gc + jnp.where(rc >= s, pltpu.roll(gc, s, axis=0), 0.0)
        s *= 2
    gc3 = gc.reshape(tm // CHUNK, CHUNK, LANES)
    gl = jnp.broadcast_to(gc3[:, CHUNK - 1:CHUNK, :], gc3.shape).reshape(tm, LANES)

    eg = jnp.exp(gc)
    beta0 = pltpu.roll(beta, LANES - N_HEADS, axis=1)

    def put(v, at):
        v = jnp.where(head_lanes, v, 0.0)
        return pltpu.roll(v, at, axis=1) if at else v

    colpack_ref[...] = (put(gc, 0) + jnp.where((lane >= 8) & (lane < 16), beta, 0.0)
                        + put(eg, 16) + put(jnp.exp(gl - gc), 24) + put(jnp.exp(gl), 32)
                        + put(beta0 * eg, 40))

    sel = (lax.broadcasted_iota(jnp.int32, (N_HEADS, LANES), 0)
           == lax.broadcasted_iota(jnp.int32, (N_HEADS, LANES), 1)).astype(BF16)
    gcm = jnp.where(head_lanes, gc, 0.0)
    hi = gcm.astype(BF16)
    r1 = gcm - hi.astype(F32)
    mid = r1.astype(BF16)
    lo = (r1 - mid.astype(F32)).astype(BF16)
    for c in range(tm // CHUNK):
        sl = slice(c * CHUNK, (c + 1) * CHUNK)
        gct_ref[c] = (_dot_nt(sel, hi[sl]) + _dot_nt(sel, mid[sl])) + _dot_nt(sel, lo[sl])


def _gdn_proj(x, gain, w, wab, gconst, conv_w, *, batch, seq, tm=512, tn=512):
    n, d = x.shape
    wout = w.shape[1]
    nt = seq // tm
    qkv = conv_w.shape[1]

    def rows(b, i):
        return (b * nt + i, 0)

    return pl.pallas_call(
        functools.partial(_gdn_proj_kernel, tn=tn),
        grid=(batch, nt),
        in_specs=[pl.BlockSpec((tm, d), rows),
                  _resident((1, d)),
                  _resident((d, wout)),
                  _resident((d, LANES)),
                  _resident((8, LANES)),
                  _resident((GDN_CONV, qkv))],
        out_specs=[pl.BlockSpec((tm, wout), rows),
                   pl.BlockSpec((tm, LANES), rows),
                   pl.BlockSpec((tm // CHUNK, N_HEADS, CHUNK), lambda b, i: (b * nt + i, 0, 0))],
        out_shape=[jax.ShapeDtypeStruct((n, wout), BF16),
                   jax.ShapeDtypeStruct((n, LANES), F32),
                   jax.ShapeDtypeStruct((n // CHUNK, N_HEADS, CHUNK), F32)],
        scratch_shapes=[pltpu.VMEM((8, qkv), F32)],
        compiler_params=_cparams(("parallel", "arbitrary")),
        name="gdn_proj",
    )(x, gain, w, wab, gconst, conv_w)


def _unit_lower_inverse(nms, masks):
    eye, m_base, level_masks = masks
    nbs = [jnp.where(m_base, nm, 0.0) for nm in nms]
    xs = [eye - nb for nb in nbs]
    nb16 = [nb.astype(BF16) for nb in nbs]
    ps = [_dot(a, a) for a in nb16]
    s = 2
    while s < INV_BASE:
        p16 = [p.astype(BF16) for p in ps]
        s *= 2
        if s < INV_BASE:
            both = [_dot(a, jnp.concatenate([a, x.astype(BF16)], axis=1)) for a, x in zip(p16, xs)]
            ps = [b[:, :CHUNK] for b in both]
            xs = [x + b[:, CHUNK:] for x, b in zip(xs, both)]
        else:
            xs = [x + _dot(a, x.astype(BF16)) for a, x in zip(p16, xs)]
    for m_off in level_masks:
        x16 = [x.astype(BF16) for x in xs]
        ws = [_dot(jnp.where(m_off, nm, 0.0).astype(BF16), a) for nm, a in zip(nms, x16)]
        xs = [x - _dot(a, w.astype(BF16)) for x, a, w in zip(xs, x16, ws)]
    return xs


def _gdn_chunk_kernel(q_ref, k_ref, v_ref, z_ref, colpack_ref, gct_ref, normw_ref,
                      o_ref, state_ref, u_ref, w_ref, qd_ref, kt_ref, qk_ref, *, prep_chunks):
    t = pl.program_id(1)
    tt = q_ref.shape[0]
    n_chunks = tt // CHUNK

    @pl.when(t == 0)
    def _():
        state_ref[...] = jnp.zeros_like(state_ref)

    ri = lax.broadcasted_iota(jnp.int32, (CHUNK, CHUNK), 0)
    ci = lax.broadcasted_iota(jnp.int32, (CHUNK, CHUNK), 1)
    causal = ri >= ci
    strict = ri > ci
    eye = (ri == ci).astype(F32)

    def same_block(sz):
        shift = sz.bit_length() - 1
        return (ri >> shift) == (ci >> shift)

    level_masks = []
    sz = INV_BASE
    while sz < CHUNK:
        level_masks.append(same_block(2 * sz) & jnp.logical_not(same_block(sz)))
        sz *= 2
    masks = (eye, same_block(INV_BASE), tuple(level_masks))
    normw = normw_ref[...]

    def hcols(h):
        return slice(h * GDN_DK, (h + 1) * GDN_DK)

    def crows(c):
        return slice(c * CHUNK, (c + 1) * CHUNK)

    for c0 in range(0, n_chunks, prep_chunks):
        pairs = [(c, h) for c in range(c0, c0 + prep_chunks) for h in range(N_HEADS)]
        cps = {c: colpack_ref[crows(c), :] for c in range(c0, c0 + prep_chunks)}
        gts = {c: gct_ref[c] for c in range(c0, c0 + prep_chunks)}
        q16 = [q_ref[crows(c), hcols(h)] for c, h in pairs]
        k16 = [k_ref[crows(c), hcols(h)] for c, h in pairs]
        qk_kk = [_dot_nt(jnp.concatenate([qb, kb], axis=0), kb)
                 for qb, kb in zip(q16, k16)]
        decay = [jnp.exp(jnp.where(causal, cps[c][:, h:h + 1] - gts[c][h:h + 1, :], -jnp.inf))
                 for c, h in pairs]
        for (c, h), a, d in zip(pairs, qk_kk, decay):
            qk_ref[crows(c), h * CHUNK:(h + 1) * CHUNK] = (a[:CHUNK] * d).astype(BF16)
        nms = [jnp.where(strict, a[CHUNK:] * cps[c][:, 8 + h:9 + h] * d, 0.0)
               for (c, h), a, d in zip(pairs, qk_kk, decay)]
        tinv = _unit_lower_inverse(nms, masks)
        qs = [qb.astype(F32) for qb in q16]
        ks = [kb.astype(F32) for kb in k16]
        rhs = [jnp.concatenate([v_ref[crows(c), hcols(h)].astype(F32) * cps[c][:, 8 + h:9 + h],
                                k * cps[c][:, 40 + h:41 + h]], axis=1).astype(BF16)
               for (c, h), k in zip(pairs, ks)]
        sol = [_dot(ti.astype(BF16), r) for ti, r in zip(tinv, rhs)]
        for (c, h), s_, q, k in zip(pairs, sol, qs, ks):
            u_ref[crows(c), hcols(h)] = s_[:, :GDN_DK]
            w_ref[crows(c), hcols(h)] = s_[:, GDN_DK:].astype(BF16)
            qd_ref[crows(c), hcols(h)] = (q * cps[c][:, 16 + h:17 + h]).astype(BF16)
            kt_ref[crows(c), hcols(h)] = (k * cps[c][:, 24 + h:25 + h]).astype(BF16)

    heads = range(N_HEADS)
    states = [state_ref[h] for h in heads]
    for c in range(n_chunks):
        rows = crows(c)
        s16 = [s_.astype(BF16) for s_ in states]
        ws_qs = [_dot(jnp.concatenate([w_ref[rows, hcols(h)], qd_ref[rows, hcols(h)]], axis=0),
                      s16[h]) for h in heads]
        vn16 = [(u_ref[rows, hcols(h)] - ws_qs[h][:CHUNK]).astype(BF16) for h in heads]
        outs = [ws_qs[h][CHUNK:] + _dot(qk_ref[rows, h * CHUNK:(h + 1) * CHUNK], vn16[h])
                for h in heads]
        elast = colpack_ref[c * CHUNK:c * CHUNK + 1, :]
        states = [states[h] * elast[:, 32 + h:33 + h] + _dot_tn(kt_ref[rows, hcols(h)], vn16[h])
                  for h in heads]
        for h in heads:
            o = outs[h]
            o = o * lax.rsqrt(jnp.mean(o * o, axis=-1, keepdims=True) + RMS_EPS)
            z = z_ref[rows, hcols(h)].astype(F32)
            o_ref[rows, hcols(h)] = (o * normw * (z * _sigmoid(z))).astype(o_ref.dtype)
    for h in heads:
        state_ref[h] = states[h]


def _gdn_chunk(proj, colpack, gct, norm_w, *, batch, seq, tt=256, prep_chunks=4):
    n = batch * seq
    nt = seq // tt
    width = N_HEADS * GDN_DK

    def col_block(cb):
        return pl.BlockSpec((tt, width), lambda b, t: (b * nt + t, cb))

    return pl.pallas_call(
        functools.partial(_gdn_chunk_kernel, prep_chunks=prep_chunks),
        grid=(batch, nt),
        in_specs=[col_block(0), col_block(1), col_block(2), col_block(3),
                  pl.BlockSpec((tt, LANES), lambda b, t: (b * nt + t, 0)),
                  pl.BlockSpec((tt // CHUNK, N_HEADS, CHUNK), lambda b, t: (b * nt + t, 0, 0)),
                  _resident((1, GDN_DK))],
        out_specs=pl.BlockSpec((tt, width), lambda b, t: (b * nt + t, 0)),
        out_shape=jax.ShapeDtypeStruct((n, width), BF16),
        scratch_shapes=[pltpu.VMEM((N_HEADS, GDN_DK, GDN_DK), F32),
                        pltpu.VMEM((tt, width), F32),
                        pltpu.VMEM((tt, width), BF16),
                        pltpu.VMEM((tt, width), BF16),
                        pltpu.VMEM((tt, width), BF16),
                        pltpu.VMEM((tt, N_HEADS * CHUNK), BF16)],
        compiler_params=_cparams(("parallel", "arbitrary")),
        name="gdn_chunk",
    )(proj, proj, proj, proj, colpack, gct, norm_w)


def _att_group_kernel(q_ref, k_ref, v_ref, qw_ref, kw_ref, o_ref, lse_ref, kn_ref, vt_ref,
                      bias_ref, *, dilation, slopes, nq):
    n = pl.program_id(2)
    span = ATT_SPAN
    rows_per_step = nq * span
    n_pairs = N_HEADS // 2

    @pl.when(n == 0)
    def _():
        kn_ref[0:span, :] = jnp.zeros((span, ATT_WIDTH), BF16)
        vt_ref[:, :, 0:span] = jnp.zeros((n_pairs, LANES + ONES_ROWS, span), BF16)

    bi = lax.broadcasted_iota(jnp.int32, (LANES, LANES), 0) // ATT_DH
    bj = lax.broadcasted_iota(jnp.int32, (LANES, LANES), 1) // ATT_DH
    half_ones = (bi == bj).astype(BF16)

    def head_norm(x16, w):
        x = x16.astype(F32)
        ms = _dot((x * x).astype(BF16), half_ones) * (1.0 / ATT_DH)
        return (x * lax.rsqrt(ms + RMS_EPS) * w).astype(BF16)

    q_gain = ATT_DH ** -0.5 * LOG2_E
    qn = []
    for p in range(n_pairs):
        cols = slice(p * LANES, (p + 1) * LANES)
        kn_ref[span:, cols] = head_norm(k_ref[:, cols], kw_ref[:, cols])
        qn.append(head_norm(q_ref[:, cols], qw_ref[:, cols] * q_gain))
        vt_ref[p, :LANES, span:] = v_ref[:, cols].astype(F32).T.astype(BF16)
        vt_ref[p, LANES:, span:] = jnp.ones((ONES_ROWS, rows_per_step), BF16)

    kj = lax.broadcasted_iota(jnp.int32, (2 * span, 2 * span), 0)
    ql = lax.broadcasted_iota(jnp.int32, (2 * span, 2 * span), 1)
    steps = (ql & (span - 1)) + span - kj
    band = (steps >= 0) & (steps <= span)
    dist = (steps * dilation).astype(F32)
    for p in range(n_pairs):
        slope = jnp.where(ql < span, slopes[2 * p], slopes[2 * p + 1])
        bias_ref[p] = jnp.where(band, -(slope * LOG2_E) * dist, -jnp.inf)

    lane_s = lax.broadcasted_iota(jnp.int32, (span, LANES), 1)
    left_s = lane_s < ATT_DH
    row_s = lax.broadcasted_iota(jnp.int32, (span, LANES), 0)

    first_key = jnp.where(n == 0, span, 0)
    pairs = range(n_pairs)
    for i in range(nq):
        qrows = slice(i * span, (i + 1) * span)
        krows = slice(i * span, (i + 2) * span)
        q_both = []
        for p in pairs:
            q_pair = qn[p][qrows]
            zero = jnp.zeros_like(q_pair)
            q_both.append(jnp.concatenate([jnp.where(left_s, q_pair, zero),
                                           jnp.where(left_s, zero, q_pair)], axis=0))
        ss = [_dot_nt(kn_ref[krows, p * LANES:(p + 1) * LANES], q_both[p]) + bias_ref[p]
              for p in pairs]
        if i == 0:
            ss = [jnp.where(kj >= first_key, s, -jnp.inf) for s in ss]
        ms = [jnp.max(s, axis=0, keepdims=True) for s in ss]
        prs = [jnp.exp2(s - m).astype(BF16) for s, m in zip(ss, ms)]
        accs = [_dot(vt_ref[p, :, krows], prs[p]) for p in pairs]
        lse_t = jnp.zeros((span, LANES), F32)
        for p in pairs:
            l = accs[p][LANES:LANES + 1, :]
            inv = 1.0 / l
            o_t = jnp.concatenate([accs[p][:ATT_DH, :span] * inv[:, :span],
                                   accs[p][ATT_DH:LANES, span:] * inv[:, span:]], axis=0)
            o_ref[qrows, p * LANES:(p + 1) * LANES] = o_t.T.astype(o_ref.dtype)
            lse = ms[p] * LN_2 + jnp.log(l)
            lse_t = jnp.where(row_s == 2 * p, lse[:, :span], lse_t)
            lse_t = jnp.where(row_s == 2 * p + 1, lse[:, span:], lse_t)
        lse_ref[qrows, :] = lse_t.T

    kn_ref[0:span, :] = kn_ref[rows_per_step:, :]
    vt_ref[:, :, 0:span] = vt_ref[:, :, rows_per_step:]


def _att_group(q, k, v, qw, kw, *, gi, batch, seq):
    window, d = ATT_GROUPS[gi]
    assert window // d == ATT_SPAN
    n_groups = len(ATT_GROUPS)
    sub = seq // d
    nblk = sub // ATT_SPAN
    nq = min(4, nblk)
    rows = nq * ATT_SPAN
    slopes = tuple(2.0 ** (-ALIBI_MAX_BIAS * (gi * N_HEADS + h + 1) / (n_groups * N_HEADS))
                   for h in range(N_HEADS))

    rows_block = pl.BlockSpec((None, None, rows, ATT_WIDTH), lambda b, r, n: (b, r, n, 0))

    return pl.pallas_call(
        functools.partial(_att_group_kernel, dilation=d, slopes=slopes, nq=nq),
        grid=(batch, d, nblk // nq),
        in_specs=[rows_block, rows_block, rows_block,
                  _resident((1, ATT_WIDTH)), _resident((1, ATT_WIDTH))],
        out_specs=[rows_block,
                   pl.BlockSpec((None, None, rows, LANES), lambda b, r, n: (b, r, n, 0))],
        out_shape=[jax.ShapeDtypeStruct((batch, d, sub, ATT_WIDTH), BF16),
                   jax.ShapeDtypeStruct((batch, d, sub, LANES), F32)],
        scratch_shapes=[pltpu.VMEM((ATT_SPAN + rows, ATT_WIDTH), BF16),
                        pltpu.VMEM((N_HEADS // 2, LANES + ONES_ROWS, ATT_SPAN + rows), BF16),
                        pltpu.VMEM((N_HEADS // 2, 2 * ATT_SPAN, 2 * ATT_SPAN), F32)],
        compiler_params=_cparams(("parallel", "parallel", "arbitrary")),
        name=f"att_group{gi}",
    )(q, k, v, qw, kw)


def _ffn_tail(x, mixed16, wmix_ref, gain_ref, win_ref, wout_ref, y_ref, th):
    hidden = wout_ref.shape[0]
    x1 = x + _dot(mixed16, wmix_ref[...])
    xn = _rms_scale(x1, gain_ref[...]).astype(BF16)
    acc = x1
    for j in range(hidden // th):
        gate = _dot(xn, win_ref[:, j * th:(j + 1) * th])
        up = _dot(xn, win_ref[:, hidden + j * th:hidden + (j + 1) * th])
        act = (gate * _sigmoid(gate) * up).astype(BF16)
        acc = acc + _dot(act, wout_ref[j * th:(j + 1) * th, :])
    y_ref[...] = acc


def _out_ffn_kernel(x_ref, o_ref, wmix_ref, gain_ref, win_ref, wout_ref, y_ref, *, th):
    _ffn_tail(x_ref[...], o_ref[...], wmix_ref, gain_ref, win_ref, wout_ref, y_ref, th)


def _out_ffn(x, o, wmix, gain, win, wout, *, tm=1024, th=256):
    n, d = x.shape
    ko = o.shape[1]
    hidden = wout.shape[0]
    return pl.pallas_call(
        functools.partial(_out_ffn_kernel, th=th),
        grid=(n // tm,),
        in_specs=[pl.BlockSpec((tm, d), lambda i: (i, 0)),
                  pl.BlockSpec((tm, ko), lambda i: (i, 0)),
                  _resident((ko, d)),
                  _resident((1, d)),
                  _resident((d, 2 * hidden)),
                  _resident((hidden, d))],
        out_specs=pl.BlockSpec((tm, d), lambda i: (i, 0)),
        out_shape=jax.ShapeDtypeStruct((n, d), F32),
        compiler_params=_cparams(("parallel",)),
        name="out_ffn",
    )(x, o, wmix, gain, win, wout)


def _merge_out_ffn_kernel(x_ref, o0_ref, o1_ref, o2_ref, l0_ref, l1_ref, l2_ref,
                          wmix_ref, gain_ref, win_ref, wout_ref, y_ref, onat_ref, lnat_ref, *, th):
    tm = x_ref.shape[0]
    n_pairs = N_HEADS // 2
    outs, lses = [], []
    for g, ((window, d), o_ref, l_ref) in enumerate(zip(ATT_GROUPS, (o0_ref, o1_ref, o2_ref),
                                                        (l0_ref, l1_ref, l2_ref))):
        if d == 1:
            outs.append([o_ref[0, :, p * LANES:(p + 1) * LANES].astype(F32) for p in range(n_pairs)])
            lses.append(l_ref[0])
            continue
        rows = tm // d
        for r in range(d):
            for p in range(n_pairs):
                onat_ref[g - 1, p, pl.ds(r, rows, stride=d), :] = (
                    o_ref[r, :, p * LANES:(p + 1) * LANES].astype(F32))
            lnat_ref[g - 1, pl.ds(r, rows, stride=d), :] = l_ref[r]
        outs.append([onat_ref[g - 1, p] for p in range(n_pairs)])
        lses.append(lnat_ref[g - 1])
    mx = jnp.maximum(jnp.maximum(lses[0], lses[1]), lses[2])
    wts = [jnp.exp(l - mx) for l in lses]
    inv = 1.0 / (wts[0] + wts[1] + wts[2])
    wts = [w * inv for w in wts]
    left = lax.broadcasted_iota(jnp.int32, (tm, LANES), 1) < ATT_DH
    pairs = []
    for p in range(n_pairs):
        acc = None
        for w, o in zip(wts, outs):
            term = jnp.where(left, w[:, 2 * p:2 * p + 1], w[:, 2 * p + 1:2 * p + 2]) * o[p]
            acc = term if acc is None else acc + term
        pairs.append(acc.astype(BF16))
    mixed16 = jnp.concatenate(pairs, axis=1)
    _ffn_tail(x_ref[...], mixed16, wmix_ref, gain_ref, win_ref, wout_ref, y_ref, th)


def _merge_out_ffn(x, outs, lses, wmix, gain, win, wout, *, batch, seq, tm=512, th=256):
    n, d_model = x.shape
    nt = seq // tm
    hidden = wout.shape[0]
    o_specs = [pl.BlockSpec((None, d, tm // d, ATT_WIDTH), lambda b, i: (b, 0, i, 0))
               for window, d in ATT_GROUPS]
    l_specs = [pl.BlockSpec((None, d, tm // d, LANES), lambda b, i: (b, 0, i, 0))
               for window, d in ATT_GROUPS]
    return pl.pallas_call(
        functools.partial(_merge_out_ffn_kernel, th=th),
        grid=(batch, nt),
        in_specs=[pl.BlockSpec((tm, d_model), lambda b, i: (b * nt + i, 0))] + o_specs + l_specs
                 + [_resident((ATT_WIDTH, d_model)), _resident((1, d_model)),
                    _resident((d_model, 2 * hidden)), _resident((hidden, d_model))],
        out_specs=pl.BlockSpec((tm, d_model), lambda b, i: (b * nt + i, 0)),
        out_shape=jax.ShapeDtypeStruct((n, d_model), F32),
        scratch_shapes=[pltpu.VMEM((len(ATT_GROUPS) - 1, ATT_WIDTH // LANES, tm, LANES), F32),
                        pltpu.VMEM((len(ATT_GROUPS) - 1, tm, LANES), F32)],
        compiler_params=_cparams(("parallel", "parallel")),
        name="merge_out_ffn",
    )(x, *outs, *lses, wmix, gain, win, wout)


def _gdn_layer(xf, gain_mix, w_in, conv_w, a_log, dt_bias, norm_w, *, batch, seq):
    d_model = xf.shape[1]
    main = 4 * N_HEADS * GDN_DK
    wab = jnp.zeros((d_model, LANES), BF16).at[:, :2 * N_HEADS].set(w_in[:, main:].astype(BF16))
    gconst = jnp.zeros((8, LANES), F32)
    gconst = gconst.at[0, :N_HEADS].set(-jnp.exp(a_log.astype(F32)))
    gconst = gconst.at[1, :N_HEADS].set(dt_bias.astype(F32))
    proj, colpack, gct = _gdn_proj(xf, gain_mix, w_in[:, :main].astype(BF16), wab, gconst,
                                   conv_w.astype(F32), batch=batch, seq=seq)
    return _gdn_chunk(proj, colpack, gct, norm_w.reshape(1, GDN_DK).astype(F32),
                      batch=batch, seq=seq)


def _att_layer(xf, gain_mix, w_in, q_norm, k_norm, *, batch, seq):
    n_groups = len(ATT_GROUPS)
    w_in = w_in.astype(BF16)
    ws = [jnp.concatenate([w_in[:, (which * n_groups + gi) * ATT_WIDTH:
                                (which * n_groups + gi + 1) * ATT_WIDTH]
                           for which in range(3)], axis=1) for gi in range(n_groups)]
    projs = _dil_proj(xf, gain_mix, ws, batch=batch, seq=seq)
    qw = jnp.tile(q_norm.astype(F32), N_HEADS).reshape(1, ATT_WIDTH)
    kw = jnp.tile(k_norm.astype(F32), N_HEADS).reshape(1, ATT_WIDTH)
    res = [_att_group(*projs[3 * gi:3 * gi + 3], qw, kw, gi=gi, batch=batch, seq=seq)
           for gi in range(n_groups)]
    return [r[0] for r in res], [r[1] for r in res]


def kernel(x, norm_mix, norm_ffn, gdn_w_in, gdn_conv_w, gdn_a_log, gdn_dt_bias, gdn_norm_w,
           gdn_w_out, dil_w_in, dil_q_norm, dil_k_norm, dil_w_out, ffn_w_in, ffn_w_out):
    batch, seq, d_model = x.shape
    depth = norm_mix.shape[0]
    xf = x.reshape(batch * seq, d_model)
    for i in range(depth):
        j = i // 2
        gain_mix = norm_mix[i].reshape(1, d_model)
        ffn = (norm_ffn[i].reshape(1, d_model), ffn_w_in[i].astype(BF16), ffn_w_out[i].astype(BF16))
        if i % 2 == 0:
            mixed = _gdn_layer(xf, gain_mix, gdn_w_in[j], gdn_conv_w[j], gdn_a_log[j],
                               gdn_dt_bias[j], gdn_norm_w[j], batch=batch, seq=seq)
            xf = _out_ffn(xf, mixed, gdn_w_out[j].astype(BF16), *ffn)
        else:
            outs, lses = _att_layer(xf, gain_mix, dil_w_in[j], dil_q_norm[j], dil_k_norm[j],
                                    batch=batch, seq=seq)
            xf = _merge_out_ffn(xf, outs, lses, dil_w_out[j].astype(BF16), *ffn,
                                batch=batch, seq=seq)
    return xf.reshape(batch, seq, d_model)
```

```python
import functools

import jax
import jax.numpy as jnp
from jax import lax
from jax.experimental import pallas as pl
from jax.experimental.pallas import tpu as pltpu

F32 = jnp.float32
BF16 = jnp.bfloat16

RMS_EPS = 1e-6
L2_EPS = 1e-6

N_HEADS = 8
GDN_DK = 128
GDN_CONV = 4
CHUNK = 128
INV_BASE = 8

ATT_GROUPS = ((128, 1), (512, 4), (2048, 16))
ATT_SPAN = 128
ATT_DH = 64
ATT_WIDTH = N_HEADS * ATT_DH
ALIBI_MAX_BIAS = 8.0
LOG2_E = 1.4426950408889634
LN_2 = 0.6931471805599453
ONES_ROWS = 16

LANES = 128
VMEM_LIMIT = 56 * 1024 * 1024


def _cparams(sem):
    return pltpu.CompilerParams(dimension_semantics=sem, vmem_limit_bytes=VMEM_LIMIT)


def _resident(shape):
    nd = len(shape)
    return pl.BlockSpec(shape, lambda *_: (0,) * nd, pipeline_mode=pl.Buffered(1))


def _rms_scale(x, gain):
    ms = jnp.mean(x * x, axis=-1, keepdims=True)
    return x * lax.rsqrt(ms + RMS_EPS) * gain


def _sigmoid(x):
    return 1.0 / (1.0 + jnp.exp(-x))


def _dot(a, b):
    return jnp.dot(a, b, preferred_element_type=F32)


def _dot_nt(a, b):
    return lax.dot_general(a, b, (((1,), (1,)), ((), ())), preferred_element_type=F32)


def _dot_tn(a, b):
    return lax.dot_general(a, b, (((0,), (0,)), ((), ())), preferred_element_type=F32)


def _dil_proj_kernel(x_ref, gain_ref, w0_ref, w1_ref, w2_ref, *refs):
    n_out = 3 * len(ATT_GROUPS)
    out_refs, (xn_ref, perm_ref) = refs[:n_out], refs[n_out:]
    tm = x_ref.shape[0]
    xn = _rms_scale(x_ref[...], gain_ref[...])
    n_tiles = xn.shape[1] // LANES
    for c in range(n_tiles):
        xn_ref[c] = xn[:, c * LANES:(c + 1) * LANES]
    for g, ((window, d), w_ref) in enumerate(zip(ATT_GROUPS, (w0_ref, w1_ref, w2_ref))):
        rows = tm // d
        if d == 1:
            lhs = xn.astype(BF16)
        else:
            for r in range(d):
                for c in range(n_tiles):
                    perm_ref[r * rows:(r + 1) * rows, c * LANES:(c + 1) * LANES] = (
                        xn_ref[c, pl.ds(r, rows, stride=d), :].astype(BF16))
            lhs = perm_ref[...]
        for j in range(3):
            cols = slice(j * ATT_WIDTH, (j + 1) * ATT_WIDTH)
            res = _dot(lhs, w_ref[:, cols]).astype(BF16)
            for r in range(d):
                out_refs[3 * g + j][r] = res[r * rows:(r + 1) * rows]


def _dil_proj(x, gain, ws, *, batch, seq, tm=512):
    n, d_model = x.shape
    nt = seq // tm
    out_specs, out_shape = [], []
    for window, d in ATT_GROUPS:
        for _ in range(3):
            out_specs.append(pl.BlockSpec((None, d, tm // d, ATT_WIDTH), lambda b, i: (b, 0, i, 0)))
            out_shape.append(jax.ShapeDtypeStruct((batch, d, seq // d, ATT_WIDTH), BF16))
    return pl.pallas_call(
        _dil_proj_kernel,
        grid=(batch, nt),
        in_specs=[pl.BlockSpec((tm, d_model), lambda b, i: (b * nt + i, 0)),
                  _resident((1, d_model))] + [_resident((d_model, 3 * ATT_WIDTH))] * 3,
        out_specs=out_specs,
        out_shape=out_shape,
        scratch_shapes=[pltpu.VMEM((d_model // LANES, tm, LANES), F32),
                        pltpu.VMEM((tm, d_model), BF16)],
        compiler_params=_cparams(("parallel", "parallel")),
        name="dil_proj",
    )(x, gain, *ws)


def _gdn_proj_kernel(x_ref, gain_ref, w_ref, wab_ref, gconst_ref, convw_ref,
                     proj_ref, colpack_ref, gct_ref, halo_ref, *, tn):
    tm = x_ref.shape[0]
    width = N_HEADS * GDN_DK

    @pl.when(pl.program_id(1) == 0)
    def _():
        halo_ref[...] = jnp.zeros_like(halo_ref)

    xn = _rms_scale(x_ref[...], gain_ref[...]).astype(BF16)
    for j in range(w_ref.shape[1] // tn):
        cols = slice(j * tn, (j + 1) * tn)
        res = _dot(xn, w_ref[:, cols])
        if j * tn >= 3 * width:
            proj_ref[:, cols] = res.astype(BF16)
            continue
        hist = jnp.concatenate([halo_ref[:, cols], res], axis=0)
        halo_ref[:, cols] = res[tm - 8:, :]
        y = convw_ref[GDN_CONV - 1:GDN_CONV, cols] * res
        for tap in range(GDN_CONV - 1):
            off = 8 - (GDN_CONV - 1) + tap
            y = y + convw_ref[tap:tap + 1, cols] * hist[off:off + tm, :]
        y = y * _sigmoid(y)
        for h in range(tn // GDN_DK):
            hc = slice(h * GDN_DK, (h + 1) * GDN_DK)
            yh = y[:, hc]
            if j * tn < 2 * width:
                inv = lax.rsqrt(jnp.sum(yh * yh, axis=-1, keepdims=True) + L2_EPS)
                yh = yh * (inv * (GDN_DK ** -0.5) if j * tn < width else inv)
            proj_ref[:, j * tn + h * GDN_DK:j * tn + (h + 1) * GDN_DK] = yh.astype(BF16)

    ab = _dot(xn, wab_ref[...])
    lane = lax.broadcasted_iota(jnp.int32, (tm, LANES), 1)
    row = lax.broadcasted_iota(jnp.int32, (tm, LANES), 0)
    head_lanes = lane < N_HEADS
    neg_a = gconst_ref[0:1, :]
    dt_bias = gconst_ref[1:2, :]
    sp_in = ab + dt_bias
    softplus = jnp.maximum(sp_in, 0.0) + jnp.log(1.0 + jnp.exp(-jnp.abs(sp_in)))
    g = neg_a * softplus
    beta = _sigmoid(ab)

    rc = row & (CHUNK - 1)
    gc = g
    s = 1
    while s < CHUNK:
        gc = gc + jnp.where(rc >= s, pltpu.roll(gc, s, axis=0), 0.0)
        s *= 2
    gc3 = gc.reshape(tm // CHUNK, CHUNK, LANES)
    gl = jnp.broadcast_to(gc3[:, CHUNK - 1:CHUNK, :], gc3.shape).reshape(tm, LANES)

    eg = jnp.exp(gc)
    beta0 = pltpu.roll(beta, LANES - N_HEADS, axis=1)

    def put(v, at):
        v = jnp.where(head_lanes, v, 0.0)
        return pltpu.roll(v, at, axis=1) if at else v

    colpack_ref[...] = (put(gc, 0) + jnp.where((lane >= 8) & (lane < 16), beta, 0.0)
                        + put(eg, 16) + put(jnp.exp(gl - gc), 24) + put(jnp.exp(gl), 32)
                        + put(beta0 * eg, 40))

    sel = (lax.broadcasted_iota(jnp.int32, (N_HEADS, LANES), 0)
           == lax.broadcasted_iota(jnp.int32, (N_HEADS, LANES), 1)).astype(BF16)
    gcm = jnp.where(head_lanes, gc, 0.0)
    hi = gcm.astype(BF16)
    r1 = gcm - hi.astype(F32)
    mid = r1.astype(BF16)
    lo = (r1 - mid.astype(F32)).astype(BF16)
    for c in range(tm // CHUNK):
        sl = slice(c * CHUNK, (c + 1) * CHUNK)
        gct_ref[c] = (_dot_nt(sel, hi[sl]) + _dot_nt(sel, mid[sl])) + _dot_nt(sel, lo[sl])


def _gdn_proj(x, gain, w, wab, gconst, conv_w, *, batch, seq, tm=512, tn=512):
    n, d = x.shape
    wout = w.shape[1]
    nt = seq // tm
    qkv = conv_w.shape[1]

    def rows(b, i):
        return (b * nt + i, 0)

    return pl.pallas_call(
        functools.partial(_gdn_proj_kernel, tn=tn),
        grid=(batch, nt),
        in_specs=[pl.BlockSpec((tm, d), rows),
                  _resident((1, d)),
                  _resident((d, wout)),
                  _resident((d, LANES)),
                  _resident((8, LANES)),
                  _resident((GDN_CONV, qkv))],
        out_specs=[pl.BlockSpec((tm, wout), rows),
                   pl.BlockSpec((tm, LANES), rows),
                   pl.BlockSpec((tm // CHUNK, N_HEADS, CHUNK), lambda b, i: (b * nt + i, 0, 0))],
        out_shape=[jax.ShapeDtypeStruct((n, wout), BF16),
                   jax.ShapeDtypeStruct((n, LANES), F32),
                   jax.ShapeDtypeStruct((n // CHUNK, N_HEADS, CHUNK), F32)],
        scratch_shapes=[pltpu.VMEM((8, qkv), F32)],
        compiler_params=_cparams(("parallel", "arbitrary")),
        name="gdn_proj",
    )(x, gain, w, wab, gconst, conv_w)


def _unit_lower_inverse(nms, masks):
    eye, m_base, level_masks = masks
    nbs = [jnp.where(m_base, nm, 0.0) for nm in nms]
    xs = [eye - nb for nb in nbs]
    nb16 = [nb.astype(BF16) for nb in nbs]
    ps = [_dot(a, a) for a in nb16]
    s = 2
    while s < INV_BASE:
        p16 = [p.astype(BF16) for p in ps]
        s *= 2
        if s < INV_BASE:
            both = [_dot(a, jnp.concatenate([a, x.astype(BF16)], axis=1)) for a, x in zip(p16, xs)]
            ps = [b[:, :CHUNK] for b in both]
            xs = [x + b[:, CHUNK:] for x, b in zip(xs, both)]
        else:
            xs = [x + _dot(a, x.astype(BF16)) for a, x in zip(p16, xs)]
    for m_off in level_masks:
        x16 = [x.astype(BF16) for x in xs]
        ws = [_dot(jnp.where(m_off, nm, 0.0).astype(BF16), a) for nm, a in zip(nms, x16)]
        xs = [x - _dot(a, w.astype(BF16)) for x, a, w in zip(xs, x16, ws)]
    return xs


def _gdn_chunk_kernel(q_ref, k_ref, v_ref, z_ref, colpack_ref, gct_ref, normw_ref,
                      o_ref, state_ref, u_ref, w_ref, qd_ref, kt_ref, qk_ref, *, prep_chunks):
    t = pl.program_id(1)
    tt = q_ref.shape[0]
    n_chunks = tt // CHUNK

    @pl.when(t == 0)
    def _():
        state_ref[...] = jnp.zeros_like(state_ref)

    ri = lax.broadcasted_iota(jnp.int32, (CHUNK, CHUNK), 0)
    ci = lax.broadcasted_iota(jnp.int32, (CHUNK, CHUNK), 1)
    causal = ri >= ci
    strict = ri > ci
    eye = (ri == ci).astype(F32)

    def same_block(sz):
        shift = sz.bit_length() - 1
        return (ri >> shift) == (ci >> shift)

    level_masks = []
    sz = INV_BASE
    while sz < CHUNK:
        level_masks.append(same_block(2 * sz) & jnp.logical_not(same_block(sz)))
        sz *= 2
    masks = (eye, same_block(INV_BASE), tuple(level_masks))
    normw = normw_ref[...]

    def hcols(h):
        return slice(h * GDN_DK, (h + 1) * GDN_DK)

    def crows(c):
        return slice(c * CHUNK, (c + 1) * CHUNK)

    for c0 in range(0, n_chunks, prep_chunks):
        pairs = [(c, h) for c in range(c0, c0 + prep_chunks) for h in range(N_HEADS)]
        cps = {c: colpack_ref[crows(c), :] for c in range(c0, c0 + prep_chunks)}
        gts = {c: gct_ref[c] for c in range(c0, c0 + prep_chunks)}
        q16 = [q_ref[crows(c), hcols(h)] for c, h in pairs]
        k16 = [k_ref[crows(c), hcols(h)] for c, h in pairs]
        qk_kk = [_dot_nt(jnp.concatenate([qb, kb], axis=0), kb)
                 for qb, kb in zip(q16, k16)]
        decay = [jnp.exp(jnp.where(causal, cps[c][:, h:h + 1] - gts[c][h:h + 1, :], -jnp.inf))
                 for c, h in pairs]
        for (c, h), a, d in zip(pairs, qk_kk, decay):
            qk_ref[crows(c), h * CHUNK:(h + 1) * CHUNK] = (a[:CHUNK] * d).astype(BF16)
        nms = [jnp.where(strict, a[CHUNK:] * cps[c][:, 8 + h:9 + h] * d, 0.0)
               for (c, h), a, d in zip(pairs, qk_kk, decay)]
        tinv = _unit_lower_inverse(nms, masks)
        qs = [qb.astype(F32) for qb in q16]
        ks = [kb.astype(F32) for kb in k16]
        rhs = [jnp.concatenate([v_ref[crows(c), hcols(h)].astype(F32) * cps[c][:, 8 + h:9 + h],
                                k * cps[c][:, 40 + h:41 + h]], axis=1).astype(BF16)
               for (c, h), k in zip(pairs, ks)]
        sol = [_dot(ti.astype(BF16), r) for ti, r in zip(tinv, rhs)]
        for (c, h), s_, q, k in zip(pairs, sol, qs, ks):
            u_ref[crows(c), hcols(h)] = s_[:, :GDN_DK]
            w_ref[crows(c), hcols(h)] = s_[:, GDN_DK:].astype(BF16)
            qd_ref[crows(c), hcols(h)] = (q * cps[c][:, 16 + h:17 + h]).astype(BF16)
            kt_ref[crows(c), hcols(h)] = (k * cps[c][:, 24 + h:25 + h]).astype(BF16)

    heads = range(N_HEADS)
    states = [state_ref[h] for h in heads]
    for c in range(n_chunks):
        rows = crows(c)
        s16 = [s_.astype(BF16) for s_ in states]
        ws_qs = [_dot(jnp.concatenate([w_ref[rows, hcols(h)], qd_ref[rows, hcols(h)]], axis=0),
                      s16[h]) for h in heads]
        vn16 = [(u_ref[rows, hcols(h)] - ws_qs[h][:CHUNK]).astype(BF16) for h in heads]
        outs = [ws_qs[h][CHUNK:] + _dot(qk_ref[rows, h * CHUNK:(h + 1) * CHUNK], vn16[h])
                for h in heads]
        elast = colpack_ref[c * CHUNK:c * CHUNK + 1, :]
        states = [states[h] * elast[:, 32 + h:33 + h] + _dot_tn(kt_ref[rows, hcols(h)], vn16[h])
                  for h in heads]
        for h in heads:
            o = outs[h]
            o = o * lax.rsqrt(jnp.mean(o * o, axis=-1, keepdims=True) + RMS_EPS)
            z = z_ref[rows, hcols(h)].astype(F32)
            o_ref[rows, hcols(h)] = (o * normw * (z * _sigmoid(z))).astype(o_ref.dtype)
    for h in heads:
        state_ref[h] = states[h]


def _gdn_chunk(proj, colpack, gct, norm_w, *, batch, seq, tt=1024, prep_chunks=2):
    n = batch * seq
    nt = seq // tt
    width = N_HEADS * GDN_DK

    def col_block(cb):
        return pl.BlockSpec((tt, width), lambda b, t: (b * nt + t, cb))

    return pl.pallas_call(
        functools.partial(_gdn_chunk_kernel, prep_chunks=prep_chunks),
        grid=(batch, nt),
        in_specs=[col_block(0), col_block(1), col_block(2), col_block(3),
                  pl.BlockSpec((tt, LANES), lambda b, t: (b * nt + t, 0)),
                  pl.BlockSpec((tt // CHUNK, N_HEADS, CHUNK), lambda b, t: (b * nt + t, 0, 0)),
                  _resident((1, GDN_DK))],
        out_specs=pl.BlockSpec((tt, width), lambda b, t: (b * nt + t, 0)),
        out_shape=jax.ShapeDtypeStruct((n, width), BF16),
        scratch_shapes=[pltpu.VMEM((N_HEADS, GDN_DK, GDN_DK), F32),
                        pltpu.VMEM((tt, width), F32),
                        pltpu.VMEM((tt, width), BF16),
                        pltpu.VMEM((tt, width), BF16),
                        pltpu.VMEM((tt, width), BF16),
                        pltpu.VMEM((tt, N_HEADS * CHUNK), BF16)],
        compiler_params=_cparams(("parallel", "arbitrary")),
        name="gdn_chunk",
    )(proj, proj, proj, proj, colpack, gct, norm_w)


def _att_group_kernel(q_ref, k_ref, v_ref, qw_ref, kw_ref, o_ref, lse_ref, kn_ref, vt_ref,
                      bias_ref, *, dilation, slopes, nq):
    n = pl.program_id(2)
    span = ATT_SPAN
    rows_per_step = nq * span
    n_pairs = N_HEADS // 2

    @pl.when(n == 0)
    def _():
        kn_ref[0:span, :] = jnp.zeros((span, ATT_WIDTH), BF16)
        vt_ref[:, :, 0:span] = jnp.zeros((n_pairs, LANES + ONES_ROWS, span), BF16)

    bi = lax.broadcasted_iota(jnp.int32, (LANES, LANES), 0) // ATT_DH
    bj = lax.broadcasted_iota(jnp.int32, (LANES, LANES), 1) // ATT_DH
    half_ones = (bi == bj).astype(BF16)

    def head_norm(x16, w):
        x = x16.astype(F32)
        ms = _dot((x * x).astype(BF16), half_ones) * (1.0 / ATT_DH)
        return (x * lax.rsqrt(ms + RMS_EPS) * w).astype(BF16)

    q_gain = ATT_DH ** -0.5 * LOG2_E
    qn = []
    for p in range(n_pairs):
        cols = slice(p * LANES, (p + 1) * LANES)
        kn_ref[span:, cols] = head_norm(k_ref[:, cols], kw_ref[:, cols])
        qn.append(head_norm(q_ref[:, cols], qw_ref[:, cols] * q_gain))
        vt_ref[p, :LANES, span:] = v_ref[:, cols].astype(F32).T.astype(BF16)
        vt_ref[p, LANES:, span:] = jnp.ones((ONES_ROWS, rows_per_step), BF16)

    kj = lax.broadcasted_iota(jnp.int32, (2 * span, 2 * span), 0)
    ql = lax.broadcasted_iota(jnp.int32, (2 * span, 2 * span), 1)
    steps = (ql & (span - 1)) + span - kj
    band = (steps >= 0) & (steps <= span)
    dist = (steps * dilation).astype(F32)
    for p in range(n_pairs):
        slope = jnp.where(ql < span, slopes[2 * p], slopes[2 * p + 1])
        bias_ref[p] = jnp.where(band, -(slope * LOG2_E) * dist, -jnp.inf)

    lane_s = lax.broadcasted_iota(jnp.int32, (span, LANES), 1)
    left_s = lane_s < ATT_DH
    row_s = lax.broadcasted_iota(jnp.int32, (span, LANES), 0)

    first_key = jnp.where(n == 0, span, 0)
    pairs = range(n_pairs)
    for i in range(nq):
        qrows = slice(i * span, (i + 1) * span)
        krows = slice(i * span, (i + 2) * span)
        q_both = []
        for p in pairs:
            q_pair = qn[p][qrows]
            zero = jnp.zeros_like(q_pair)
            q_both.append(jnp.concatenate([jnp.where(left_s, q_pair, zero),
                                           jnp.where(left_s, zero, q_pair)], axis=0))
        ss = [_dot_nt(kn_ref[krows, p * LANES:(p + 1) * LANES], q_both[p]) + bias_ref[p]
              for p in pairs]
        if i == 0:
            ss = [jnp.where(kj >= first_key, s, -jnp.inf) for s in ss]
        ms = [jnp.max(s, axis=0, keepdims=True) for s in ss]
        prs = [jnp.exp2(s - m).astype(BF16) for s, m in zip(ss, ms)]
        accs = [_dot(vt_ref[p, :, krows], prs[p]) for p in pairs]
        lse_t = jnp.zeros((span, LANES), F32)
        for p in pairs:
            l = accs[p][LANES:LANES + 1, :]
            inv = 1.0 / l
            o_t = jnp.concatenate([accs[p][:ATT_DH, :span] * inv[:, :span],
                                   accs[p][ATT_DH:LANES, span:] * inv[:, span:]], axis=0)
            o_ref[qrows, p * LANES:(p + 1) * LANES] = o_t.T.astype(o_ref.dtype)
            lse = ms[p] * LN_2 + jnp.log(l)
            lse_t = jnp.where(row_s == 2 * p, lse[:, :span], lse_t)
            lse_t = jnp.where(row_s == 2 * p + 1, lse[:, span:], lse_t)
        lse_ref[qrows, :] = lse_t.T

    kn_ref[0:span, :] = kn_ref[rows_per_step:, :]
    vt_ref[:, :, 0:span] = vt_ref[:, :, rows_per_step:]


def _att_group(q, k, v, qw, kw, *, gi, batch, seq):
    window, d = ATT_GROUPS[gi]
    assert window // d == ATT_SPAN
    n_groups = len(ATT_GROUPS)
    sub = seq // d
    nblk = sub // ATT_SPAN
    nq = min(4, nblk)
    rows = nq * ATT_SPAN
    slopes = tuple(2.0 ** (-ALIBI_MAX_BIAS * (gi * N_HEADS + h + 1) / (n_groups * N_HEADS))
                   for h in range(N_HEADS))

    rows_block = pl.BlockSpec((None, None, rows, ATT_WIDTH), lambda b, r, n: (b, r, n, 0))

    return pl.pallas_call(
        functools.partial(_att_group_kernel, dilation=d, slopes=slopes, nq=nq),
        grid=(batch, d, nblk // nq),
        in_specs=[rows_block, rows_block, rows_block,
                  _resident((1, ATT_WIDTH)), _resident((1, ATT_WIDTH))],
        out_specs=[rows_block,
                   pl.BlockSpec((None, None, rows, LANES), lambda b, r, n: (b, r, n, 0))],
        out_shape=[jax.ShapeDtypeStruct((batch, d, sub, ATT_WIDTH), BF16),
                   jax.ShapeDtypeStruct((batch, d, sub, LANES), F32)],
        scratch_shapes=[pltpu.VMEM((ATT_SPAN + rows, ATT_WIDTH), BF16),
                        pltpu.VMEM((N_HEADS // 2, LANES + ONES_ROWS, ATT_SPAN + rows), BF16),
                        pltpu.VMEM((N_HEADS // 2, 2 * ATT_SPAN, 2 * ATT_SPAN), F32)],
        compiler_params=_cparams(("parallel", "parallel", "arbitrary")),
        name=f"att_group{gi}",
    )(q, k, v, qw, kw)


def _ffn_tail(x, mixed16, wmix_ref, gain_ref, win_ref, wout_ref, y_ref, th):
    hidden = wout_ref.shape[0]
    x1 = x + _dot(mixed16, wmix_ref[...])
    xn = _rms_scale(x1, gain_ref[...]).astype(BF16)
    acc = x1
    for j in range(hidden // th):
        gate = _dot(xn, win_ref[:, j * th:(j + 1) * th])
        up = _dot(xn, win_ref[:, hidden + j * th:hidden + (j + 1) * th])
        act = (gate * _sigmoid(gate) * up).astype(BF16)
        acc = acc + _dot(act, wout_ref[j * th:(j + 1) * th, :])
    y_ref[...] = acc


def _out_ffn_kernel(x_ref, o_ref, wmix_ref, gain_ref, win_ref, wout_ref, y_ref, *, th):
    _ffn_tail(x_ref[...], o_ref[...], wmix_ref, gain_ref, win_ref, wout_ref, y_ref, th)


def _out_ffn(x, o, wmix, gain, win, wout, *, tm=1024, th=256):
    n, d = x.shape
    ko = o.shape[1]
    hidden = wout.shape[0]
    return pl.pallas_call(
        functools.partial(_out_ffn_kernel, th=th),
        grid=(n // tm,),
        in_specs=[pl.BlockSpec((tm, d), lambda i: (i, 0)),
                  pl.BlockSpec((tm, ko), lambda i: (i, 0)),
                  _resident((ko, d)),
                  _resident((1, d)),
                  _resident((d, 2 * hidden)),
                  _resident((hidden, d))],
        out_specs=pl.BlockSpec((tm, d), lambda i: (i, 0)),
        out_shape=jax.ShapeDtypeStruct((n, d), F32),
        compiler_params=_cparams(("parallel",)),
        name="out_ffn",
    )(x, o, wmix, gain, win, wout)


def _merge_out_ffn_kernel(x_ref, o0_ref, o1_ref, o2_ref, l0_ref, l1_ref, l2_ref,
                          wmix_ref, gain_ref, win_ref, wout_ref, y_ref, onat_ref, lnat_ref, *, th):
    tm = x_ref.shape[0]
    n_pairs = N_HEADS // 2
    outs, lses = [], []
    for g, ((window, d), o_ref, l_ref) in enumerate(zip(ATT_GROUPS, (o0_ref, o1_ref, o2_ref),
                                                        (l0_ref, l1_ref, l2_ref))):
        if d == 1:
            outs.append([o_ref[0, :, p * LANES:(p + 1) * LANES].astype(F32) for p in range(n_pairs)])
            lses.append(l_ref[0])
            continue
        rows = tm // d
        for r in range(d):
            for p in range(n_pairs):
                onat_ref[g - 1, p, pl.ds(r, rows, stride=d), :] = (
                    o_ref[r, :, p * LANES:(p + 1) * LANES].astype(F32))
            lnat_ref[g - 1, pl.ds(r, rows, stride=d), :] = l_ref[r]
        outs.append([onat_ref[g - 1, p] for p in range(n_pairs)])
        lses.append(lnat_ref[g - 1])
    mx = jnp.maximum(jnp.maximum(lses[0], lses[1]), lses[2])
    wts = [jnp.exp(l - mx) for l in lses]
    inv = 1.0 / (wts[0] + wts[1] + wts[2])
    wts = [w * inv for w in wts]
    left = lax.broadcasted_iota(jnp.int32, (tm, LANES), 1) < ATT_DH
    pairs = []
    for p in range(n_pairs):
        acc = None
        for w, o in zip(wts, outs):
            term = jnp.where(left, w[:, 2 * p:2 * p + 1], w[:, 2 * p + 1:2 * p + 2]) * o[p]
            acc = term if acc is None else acc + term
        pairs.append(acc.astype(BF16))
    mixed16 = jnp.concatenate(pairs, axis=1)
    _ffn_tail(x_ref[...], mixed16, wmix_ref, gain_ref, win_ref, wout_ref, y_ref, th)


def _merge_out_ffn(x, outs, lses, wmix, gain, win, wout, *, batch, seq, tm=512, th=256):
    n, d_model = x.shape
    nt = seq // tm
    hidden = wout.shape[0]
    o_specs = [pl.BlockSpec((None, d, tm // d, ATT_WIDTH), lambda b, i: (b, 0, i, 0))
               for window, d in ATT_GROUPS]
    l_specs = [pl.BlockSpec((None, d, tm // d, LANES), lambda b, i: (b, 0, i, 0))
               for window, d in ATT_GROUPS]
    return pl.pallas_call(
        functools.partial(_merge_out_ffn_kernel, th=th),
        grid=(batch, nt),
        in_specs=[pl.BlockSpec((tm, d_model), lambda b, i: (b * nt + i, 0))] + o_specs + l_specs
                 + [_resident((ATT_WIDTH, d_model)), _resident((1, d_model)),
                    _resident((d_model, 2 * hidden)), _resident((hidden, d_model))],
        out_specs=pl.BlockSpec((tm, d_model), lambda b, i: (b * nt + i, 0)),
        out_shape=jax.ShapeDtypeStruct((n, d_model), F32),
        scratch_shapes=[pltpu.VMEM((len(ATT_GROUPS) - 1, ATT_WIDTH // LANES, tm, LANES), F32),
                        pltpu.VMEM((len(ATT_GROUPS) - 1, tm, LANES), F32)],
        compiler_params=_cparams(("parallel", "parallel")),
        name="merge_out_ffn",
    )(x, *outs, *lses, wmix, gain, win, wout)


def _gdn_layer(xf, gain_mix, w_in, conv_w, a_log, dt_bias, norm_w, *, batch, seq):
    d_model = xf.shape[1]
    main = 4 * N_HEADS * GDN_DK
    wab = jnp.zeros((d_model, LANES), BF16).at[:, :2 * N_HEADS].set(w_in[:, main:].astype(BF16))
    gconst = jnp.zeros((8, LANES), F32)
    gconst = gconst.at[0, :N_HEADS].set(-jnp.exp(a_log.astype(F32)))
    gconst = gconst.at[1, :N_HEADS].set(dt_bias.astype(F32))
    proj, colpack, gct = _gdn_proj(xf, gain_mix, w_in[:, :main].astype(BF16), wab, gconst,
                                   conv_w.astype(F32), batch=batch, seq=seq)
    return _gdn_chunk(proj, colpack, gct, norm_w.reshape(1, GDN_DK).astype(F32),
                      batch=batch, seq=seq)


def _att_layer(xf, gain_mix, w_in, q_norm, k_norm, *, batch, seq):
    n_groups = len(ATT_GROUPS)
    w_in = w_in.astype(BF16)
    ws = [jnp.concatenate([w_in[:, (which * n_groups + gi) * ATT_WIDTH:
                                (which * n_groups + gi + 1) * ATT_WIDTH]
                           for which in range(3)], axis=1) for gi in range(n_groups)]
    projs = _dil_proj(xf, gain_mix, ws, batch=batch, seq=seq)
    qw = jnp.tile(q_norm.astype(F32), N_HEADS).reshape(1, ATT_WIDTH)
    kw = jnp.tile(k_norm.astype(F32), N_HEADS).reshape(1, ATT_WIDTH)
    res = [_att_group(*projs[3 * gi:3 * gi + 3], qw, kw, gi=gi, batch=batch, seq=seq)
           for gi in range(n_groups)]
    return [r[0] for r in res], [r[1] for r in res]


def kernel(x, norm_mix, norm_ffn, gdn_w_in, gdn_conv_w, gdn_a_log, gdn_dt_bias, gdn_norm_w,
           gdn_w_out, dil_w_in, dil_q_norm, dil_k_norm, dil_w_out, ffn_w_in, ffn_w_out):
    batch, seq, d_model = x.shape
    depth = norm_mix.shape[0]
    xf = x.reshape(batch * seq, d_model)
    for i in range(depth):
        j = i // 2
        gain_mix = norm_mix[i].reshape(1, d_model)
        ffn = (norm_ffn[i].reshape(1, d_model), ffn_w_in[i].astype(BF16), ffn_w_out[i].astype(BF16))
        if i % 2 == 0:
            mixed = _gdn_layer(xf, gain_mix, gdn_w_in[j], gdn_conv_w[j], gdn_a_log[j],
                               gdn_dt_bias[j], gdn_norm_w[j], batch=batch, seq=seq)
            xf = _out_ffn(xf, mixed, gdn_w_out[j].astype(BF16), *ffn)
        else:
            outs, lses = _att_layer(xf, gain_mix, dil_w_in[j], dil_q_norm[j], dil_k_norm[j],
                                    batch=batch, seq=seq)
            xf = _merge_out_ffn(xf, outs, lses, dil_w_out[j].astype(BF16), *ffn,
                                batch=batch, seq=seq)
    return xf.reshape(batch, seq, d_model)
```

```python
import functools

import jax
import jax.numpy as jnp
from jax import lax
from jax.experimental import pallas as pl
from jax.experimental.pallas import tpu as pltpu

F32 = jnp.float32
BF16 = jnp.bfloat16

RMS_EPS = 1e-6
L2_EPS = 1e-6

N_HEADS = 8
GDN_DK = 128
GDN_CONV = 4
CHUNK = 128
INV_BASE = 8

ATT_GROUPS = ((128, 1), (512, 4), (2048, 16))
ATT_SPAN = 128
ATT_DH = 64
ATT_WIDTH = N_HEADS * ATT_DH
ALIBI_MAX_BIAS = 8.0
LOG2_E = 1.4426950408889634
LN_2 = 0.6931471805599453
ONES_ROWS = 16

LANES = 128
VMEM_LIMIT = 56 * 1024 * 1024


def _cparams(sem):
    return pltpu.CompilerParams(dimension_semantics=sem, vmem_limit_bytes=VMEM_LIMIT)


def _resident(shape):
    nd = len(shape)
    return pl.BlockSpec(shape, lambda *_: (0,) * nd, pipeline_mode=pl.Buffered(1))


def _rms_scale(x, gain):
    ms = jnp.mean(x * x, axis=-1, keepdims=True)
    return x * lax.rsqrt(ms + RMS_EPS) * gain


def _sigmoid(x):
    return 1.0 / (1.0 + jnp.exp(-x))


def _dot(a, b):
    return jnp.dot(a, b, preferred_element_type=F32)


def _dot_nt(a, b):
    return lax.dot_general(a, b, (((1,), (1,)), ((), ())), preferred_element_type=F32)


def _dot_tn(a, b):
    return lax.dot_general(a, b, (((0,), (0,)), ((), ())), preferred_element_type=F32)


def _dil_proj_kernel(x_ref, gain_ref, w0_ref, w1_ref, w2_ref, *refs):
    n_out = 3 * len(ATT_GROUPS)
    out_refs, (xn_ref, perm_ref) = refs[:n_out], refs[n_out:]
    tm = x_ref.shape[0]
    xn = _rms_scale(x_ref[...], gain_ref[...])
    n_tiles = xn.shape[1] // LANES
    for c in range(n_tiles):
        xn_ref[c] = xn[:, c * LANES:(c + 1) * LANES]
    for g, ((window, d), w_ref) in enumerate(zip(ATT_GROUPS, (w0_ref, w1_ref, w2_ref))):
        rows = tm // d
        if d == 1:
            lhs = xn.astype(BF16)
        else:
            for r in range(d):
                for c in range(n_tiles):
                    perm_ref[r * rows:(r + 1) * rows, c * LANES:(c + 1) * LANES] = (
                        xn_ref[c, pl.ds(r, rows, stride=d), :].astype(BF16))
            lhs = perm_ref[...]
        for j in range(3):
            cols = slice(j * ATT_WIDTH, (j + 1) * ATT_WIDTH)
            res = _dot(lhs, w_ref[:, cols]).astype(BF16)
            for r in range(d):
                out_refs[3 * g + j][r] = res[r * rows:(r + 1) * rows]


def _dil_proj(x, gain, ws, *, batch, seq, tm=1024):
    n, d_model = x.shape
    nt = seq // tm
    out_specs, out_shape = [], []
    for window, d in ATT_GROUPS:
        for _ in range(3):
            out_specs.append(pl.BlockSpec((None, d, tm // d, ATT_WIDTH), lambda b, i: (b, 0, i, 0)))
            out_shape.append(jax.ShapeDtypeStruct((batch, d, seq // d, ATT_WIDTH), BF16))
    return pl.pallas_call(
        _dil_proj_kernel,
        grid=(batch, nt),
        in_specs=[pl.BlockSpec((tm, d_model), lambda b, i: (b * nt + i, 0)),
                  _resident((1, d_model))] + [_resident((d_model, 3 * ATT_WIDTH))] * 3,
        out_specs=out_specs,
        out_shape=out_shape,
        scratch_shapes=[pltpu.VMEM((d_model // LANES, tm, LANES), F32),
                        pltpu.VMEM((tm, d_model), BF16)],
        compiler_params=_cparams(("parallel", "parallel")),
        name="dil_proj",
    )(x, gain, *ws)


def _gdn_proj_kernel(x_ref, gain_ref, w_ref, wab_ref, gconst_ref, convw_ref,
                     proj_ref, colpack_ref, gct_ref, halo_ref, *, tn):
    tm = x_ref.shape[0]
    width = N_HEADS * GDN_DK

    @pl.when(pl.program_id(1) == 0)
    def _():
        halo_ref[...] = jnp.zeros_like(halo_ref)

    xn = _rms_scale(x_ref[...], gain_ref[...]).astype(BF16)
    for j in range(w_ref.shape[1] // tn):
        cols = slice(j * tn, (j + 1) * tn)
        res = _dot(xn, w_ref[:, cols])
        if j * tn >= 3 * width:
            proj_ref[:, cols] = res.astype(BF16)
            continue
        hist = jnp.concatenate([halo_ref[:, cols], res], axis=0)
        halo_ref[:, cols] = res[tm - 8:, :]
        y = convw_ref[GDN_CONV - 1:GDN_CONV, cols] * res
        for tap in range(GDN_CONV - 1):
            off = 8 - (GDN_CONV - 1) + tap
            y = y + convw_ref[tap:tap + 1, cols] * hist[off:off + tm, :]
        y = y * _sigmoid(y)
        for h in range(tn // GDN_DK):
            hc = slice(h * GDN_DK, (h + 1) * GDN_DK)
            yh = y[:, hc]
            if j * tn < 2 * width:
                inv = lax.rsqrt(jnp.sum(yh * yh, axis=-1, keepdims=True) + L2_EPS)
                yh = yh * (inv * (GDN_DK ** -0.5) if j * tn < width else inv)
            proj_ref[:, j * tn + h * GDN_DK:j * tn + (h + 1) * GDN_DK] = yh.astype(BF16)

    ab = _dot(xn, wab_ref[...])
    lane = lax.broadcasted_iota(jnp.int32, (tm, LANES), 1)
    row = lax.broadcasted_iota(jnp.int32, (tm, LANES), 0)
    head_lanes = lane < N_HEADS
    neg_a = gconst_ref[0:1, :]
    dt_bias = gconst_ref[1:2, :]
    sp_in = ab + dt_bias
    softplus = jnp.maximum(sp_in, 0.0) + jnp.log(1.0 + jnp.exp(-jnp.abs(sp_in)))
    g = neg_a * softplus
    beta = _sigmoid(ab)

    rc = row & (CHUNK - 1)
    gc = g
    s = 1
    while s < CHUNK:
        gc = gc + jnp.where(rc >= s, pltpu.roll(gc, s, axis=0), 0.0)
        s *= 2
    gc3 = gc.reshape(tm // CHUNK, CHUNK, LANES)
    gl = jnp.broadcast_to(gc3[:, CHUNK - 1:CHUNK, :], gc3.shape).reshape(tm, LANES)

    eg = jnp.exp(gc)
    beta0 = pltpu.roll(beta, LANES - N_HEADS, axis=1)

    def put(v, at):
        v = jnp.where(head_lanes, v, 0.0)
        return pltpu.roll(v, at, axis=1) if at else v

    colpack_ref[...] = (put(gc, 0) + jnp.where((lane >= 8) & (lane < 16), beta, 0.0)
                        + put(eg, 16) + put(jnp.exp(gl - gc), 24) + put(jnp.exp(gl), 32)
                        + put(beta0 * eg, 40))

    sel = (lax.broadcasted_iota(jnp.int32, (N_HEADS, LANES), 0)
           == lax.broadcasted_iota(jnp.int32, (N_HEADS, LANES), 1)).astype(BF16)
    gcm = jnp.where(head_lanes, gc, 0.0)
    hi = gcm.astype(BF16)
    r1 = gcm - hi.astype(F32)
    mid = r1.astype(BF16)
    lo = (r1 - mid.astype(F32)).astype(BF16)
    for c in range(tm // CHUNK):
        sl = slice(c * CHUNK, (c + 1) * CHUNK)
        gct_ref[c] = (_dot_nt(sel, hi[sl]) + _dot_nt(sel, mid[sl])) + _dot_nt(sel, lo[sl])


def _gdn_proj(x, gain, w, wab, gconst, conv_w, *, batch, seq, tm=512, tn=512):
    n, d = x.shape
    wout = w.shape[1]
    nt = seq // tm
    qkv = conv_w.shape[1]

    def rows(b, i):
        return (b * nt + i, 0)

    return pl.pallas_call(
        functools.partial(_gdn_proj_kernel, tn=tn),
        grid=(batch, nt),
        in_specs=[pl.BlockSpec((tm, d), rows),
                  _resident((1, d)),
                  _resident((d, wout)),
                  _resident((d, LANES)),
                  _resident((8, LANES)),
                  _resident((GDN_CONV, qkv))],
        out_specs=[pl.BlockSpec((tm, wout), rows),
                   pl.BlockSpec((tm, LANES), rows),
                   pl.BlockSpec((tm // CHUNK, N_HEADS, CHUNK), lambda b, i: (b * nt + i, 0, 0))],
        out_shape=[jax.ShapeDtypeStruct((n, wout), BF16),
                   jax.ShapeDtypeStruct((n, LANES), F32),
                   jax.ShapeDtypeStruct((n // CHUNK, N_HEADS, CHUNK), F32)],
        scratch_shapes=[pltpu.VMEM((8, qkv), F32)],
        compiler_params=_cparams(("parallel", "arbitrary")),
        name="gdn_proj",
    )(x, gain, w, wab, gconst, conv_w)


def _unit_lower_inverse(nms, masks):
    eye, m_base, level_masks = masks
    nbs = [jnp.where(m_base, nm, 0.0) for nm in nms]
    xs = [eye - nb for nb in nbs]
    nb16 = [nb.astype(BF16) for nb in nbs]
    ps = [_dot(a, a) for a in nb16]
    s = 2
    while s < INV_BASE:
        p16 = [p.astype(BF16) for p in ps]
        s *= 2
        if s < INV_BASE:
            both = [_dot(a, jnp.concatenate([a, x.astype(BF16)], axis=1)) for a, x in zip(p16, xs)]
            ps = [b[:, :CHUNK] for b in both]
            xs = [x + b[:, CHUNK:] for x, b in zip(xs, both)]
        else:
            xs = [x + _dot(a, x.astype(BF16)) for a, x in zip(p16, xs)]
    for m_off in level_masks:
        x16 = [x.astype(BF16) for x in xs]
        ws = [_dot(jnp.where(m_off, nm, 0.0).astype(BF16), a) for nm, a in zip(nms, x16)]
        xs = [x - _dot(a, w.astype(BF16)) for x, a, w in zip(xs, x16, ws)]
    return xs


def _gdn_chunk_kernel(q_ref, k_ref, v_ref, z_ref, colpack_ref, gct_ref, normw_ref,
                      o_ref, state_ref, u_ref, w_ref, qd_ref, kt_ref, qk_ref, *, prep_chunks):
    t = pl.program_id(1)
    tt = q_ref.shape[0]
    n_chunks = tt // CHUNK

    @pl.when(t == 0)
    def _():
        state_ref[...] = jnp.zeros_like(state_ref)

    ri = lax.broadcasted_iota(jnp.int32, (CHUNK, CHUNK), 0)
    ci = lax.broadcasted_iota(jnp.int32, (CHUNK, CHUNK), 1)
    causal = ri >= ci
    strict = ri > ci
    eye = (ri == ci).astype(F32)

    def same_block(sz):
        shift = sz.bit_length() - 1
        return (ri >> shift) == (ci >> shift)

    level_masks = []
    sz = INV_BASE
    while sz < CHUNK:
        level_masks.append(same_block(2 * sz) & jnp.logical_not(same_block(sz)))
        sz *= 2
    masks = (eye, same_block(INV_BASE), tuple(level_masks))
    normw = normw_ref[...]

    def hcols(h):
        return slice(h * GDN_DK, (h + 1) * GDN_DK)

    def crows(c):
        return slice(c * CHUNK, (c + 1) * CHUNK)

    for c0 in range(0, n_chunks, prep_chunks):
        pairs = [(c, h) for c in range(c0, c0 + prep_chunks) for h in range(N_HEADS)]
        cps = {c: colpack_ref[crows(c), :] for c in range(c0, c0 + prep_chunks)}
        gts = {c: gct_ref[c] for c in range(c0, c0 + prep_chunks)}
        q16 = [q_ref[crows(c), hcols(h)] for c, h in pairs]
        k16 = [k_ref[crows(c), hcols(h)] for c, h in pairs]
        qk_kk = [_dot_nt(jnp.concatenate([qb, kb], axis=0), kb)
                 for qb, kb in zip(q16, k16)]
        decay = [jnp.exp(jnp.where(causal, cps[c][:, h:h + 1] - gts[c][h:h + 1, :], -jnp.inf))
                 for c, h in pairs]
        for (c, h), a, d in zip(pairs, qk_kk, decay):
            qk_ref[crows(c), h * CHUNK:(h + 1) * CHUNK] = (a[:CHUNK] * d).astype(BF16)
        nms = [jnp.where(strict, a[CHUNK:] * cps[c][:, 8 + h:9 + h] * d, 0.0)
               for (c, h), a, d in zip(pairs, qk_kk, decay)]
        tinv = _unit_lower_inverse(nms, masks)
        qs = [qb.astype(F32) for qb in q16]
        ks = [kb.astype(F32) for kb in k16]
        rhs = [jnp.concatenate([v_ref[crows(c), hcols(h)].astype(F32) * cps[c][:, 8 + h:9 + h],
                                k * cps[c][:, 40 + h:41 + h]], axis=1).astype(BF16)
               for (c, h), k in zip(pairs, ks)]
        sol = [_dot(ti.astype(BF16), r) for ti, r in zip(tinv, rhs)]
        for (c, h), s_, q, k in zip(pairs, sol, qs, ks):
            u_ref[crows(c), hcols(h)] = s_[:, :GDN_DK]
            w_ref[crows(c), hcols(h)] = s_[:, GDN_DK:].astype(BF16)
            qd_ref[crows(c), hcols(h)] = (q * cps[c][:, 16 + h:17 + h]).astype(BF16)
            kt_ref[crows(c), hcols(h)] = (k * cps[c][:, 24 + h:25 + h]).astype(BF16)

    heads = range(N_HEADS)
    states = [state_ref[h] for h in heads]
    for c in range(n_chunks):
        rows = crows(c)
        s16 = [s_.astype(BF16) for s_ in states]
        ws_qs = [_dot(jnp.concatenate([w_ref[rows, hcols(h)], qd_ref[rows, hcols(h)]], axis=0),
                      s16[h]) for h in heads]
        vn16 = [(u_ref[rows, hcols(h)] - ws_qs[h][:CHUNK]).astype(BF16) for h in heads]
        outs = [ws_qs[h][CHUNK:] + _dot(qk_ref[rows, h * CHUNK:(h + 1) * CHUNK], vn16[h])
                for h in heads]
        elast = colpack_ref[c * CHUNK:c * CHUNK + 1, :]
        states = [states[h] * elast[:, 32 + h:33 + h] + _dot_tn(kt_ref[rows, hcols(h)], vn16[h])
                  for h in heads]
        for h in heads:
            o = outs[h]
            o = o * lax.rsqrt(jnp.mean(o * o, axis=-1, keepdims=True) + RMS_EPS)
            z = z_ref[rows, hcols(h)].astype(F32)
            o_ref[rows, hcols(h)] = (o * normw * (z * _sigmoid(z))).astype(o_ref.dtype)
    for h in heads:
        state_ref[h] = states[h]


def _gdn_chunk(proj, colpack, gct, norm_w, *, batch, seq, tt=1024, prep_chunks=2):
    n = batch * seq
    nt = seq // tt
    width = N_HEADS * GDN_DK

    def col_block(cb):
        return pl.BlockSpec((tt, width), lambda b, t: (b * nt + t, cb))

    return pl.pallas_call(
        functools.partial(_gdn_chunk_kernel, prep_chunks=prep_chunks),
        grid=(batch, nt),
        in_specs=[col_block(0), col_block(1), col_block(2), col_block(3),
                  pl.BlockSpec((tt, LANES), lambda b, t: (b * nt + t, 0)),
                  pl.BlockSpec((tt // CHUNK, N_HEADS, CHUNK), lambda b, t: (b * nt + t, 0, 0)),
                  _resident((1, GDN_DK))],
        out_specs=pl.BlockSpec((tt, width), lambda b, t: (b * nt + t, 0)),
        out_shape=jax.ShapeDtypeStruct((n, width), BF16),
        scratch_shapes=[pltpu.VMEM((N_HEADS, GDN_DK, GDN_DK), F32),
                        pltpu.VMEM((tt, width), F32),
                        pltpu.VMEM((tt, width), BF16),
                        pltpu.VMEM((tt, width), BF16),
                        pltpu.VMEM((tt, width), BF16),
                        pltpu.VMEM((tt, N_HEADS * CHUNK), BF16)],
        compiler_params=_cparams(("parallel", "arbitrary")),
        name="gdn_chunk",
    )(proj, proj, proj, proj, colpack, gct, norm_w)


def _att_group_kernel(q_ref, k_ref, v_ref, qw_ref, kw_ref, o_ref, lse_ref, kn_ref, vt_ref,
                      bias_ref, *, dilation, slopes, nq):
    n = pl.program_id(2)
    span = ATT_SPAN
    rows_per_step = nq * span
    n_pairs = N_HEADS // 2

    @pl.when(n == 0)
    def _():
        kn_ref[0:span, :] = jnp.zeros((span, ATT_WIDTH), BF16)
        vt_ref[:, :, 0:span] = jnp.zeros((n_pairs, LANES + ONES_ROWS, span), BF16)

    bi = lax.broadcasted_iota(jnp.int32, (LANES, LANES), 0) // ATT_DH
    bj = lax.broadcasted_iota(jnp.int32, (LANES, LANES), 1) // ATT_DH
    half_ones = (bi == bj).astype(BF16)

    def head_norm(x16, w):
        x = x16.astype(F32)
        ms = _dot((x * x).astype(BF16), half_ones) * (1.0 / ATT_DH)
        return (x * lax.rsqrt(ms + RMS_EPS) * w).astype(BF16)

    q_gain = ATT_DH ** -0.5 * LOG2_E
    qn = []
    for p in range(n_pairs):
        cols = slice(p * LANES, (p + 1) * LANES)
        kn_ref[span:, cols] = head_norm(k_ref[:, cols], kw_ref[:, cols])
        qn.append(head_norm(q_ref[:, cols], qw_ref[:, cols] * q_gain))
        vt_ref[p, :LANES, span:] = v_ref[:, cols].astype(F32).T.astype(BF16)
        vt_ref[p, LANES:, span:] = jnp.ones((ONES_ROWS, rows_per_step), BF16)

    kj = lax.broadcasted_iota(jnp.int32, (2 * span, 2 * span), 0)
    ql = lax.broadcasted_iota(jnp.int32, (2 * span, 2 * span), 1)
    steps = (ql & (span - 1)) + span - kj
    band = (steps >= 0) & (steps <= span)
    dist = (steps * dilation).astype(F32)
    for p in range(n_pairs):
        slope = jnp.where(ql < span, slopes[2 * p], slopes[2 * p + 1])
        bias_ref[p] = jnp.where(band, -(slope * LOG2_E) * dist, -jnp.inf)

    lane_s = lax.broadcasted_iota(jnp.int32, (span, LANES), 1)
    left_s = lane_s < ATT_DH
    row_s = lax.broadcasted_iota(jnp.int32, (span, LANES), 0)

    first_key = jnp.where(n == 0, span, 0)
    pairs = range(n_pairs)
    for i in range(nq):
        qrows = slice(i * span, (i + 1) * span)
        krows = slice(i * span, (i + 2) * span)
        q_both = []
        for p in pairs:
            q_pair = qn[p][qrows]
            zero = jnp.zeros_like(q_pair)
            q_both.append(jnp.concatenate([jnp.where(left_s, q_pair, zero),
                                           jnp.where(left_s, zero, q_pair)], axis=0))
        ss = [_dot_nt(kn_ref[krows, p * LANES:(p + 1) * LANES], q_both[p]) + bias_ref[p]
              for p in pairs]
        if i == 0:
            ss = [jnp.where(kj >= first_key, s, -jnp.inf) for s in ss]
        ms = [jnp.max(s, axis=0, keepdims=True) for s in ss]
        prs = [jnp.exp2(s - m).astype(BF16) for s, m in zip(ss, ms)]
        accs = [_dot(vt_ref[p, :, krows], prs[p]) for p in pairs]
        lse_t = jnp.zeros((span, LANES), F32)
        for p in pairs:
            l = accs[p][LANES:LANES + 1, :]
            inv = 1.0 / l
            o_t = jnp.concatenate([accs[p][:ATT_DH, :span] * inv[:, :span],
                                   accs[p][ATT_DH:LANES, span:] * inv[:, span:]], axis=0)
            o_ref[qrows, p * LANES:(p + 1) * LANES] = o_t.T.astype(o_ref.dtype)
            lse = ms[p] * LN_2 + jnp.log(l)
            lse_t = jnp.where(row_s == 2 * p, lse[:, :span], lse_t)
            lse_t = jnp.where(row_s == 2 * p + 1, lse[:, span:], lse_t)
        lse_ref[qrows, :] = lse_t.T

    kn_ref[0:span, :] = kn_ref[rows_per_step:, :]
    vt_ref[:, :, 0:span] = vt_ref[:, :, rows_per_step:]


def _att_group(q, k, v, qw, kw, *, gi, batch, seq):
    window, d = ATT_GROUPS[gi]
    assert window // d == ATT_SPAN
    n_groups = len(ATT_GROUPS)
    sub = seq // d
    nblk = sub // ATT_SPAN
    nq = min(8, nblk)
    rows = nq * ATT_SPAN
    slopes = tuple(2.0 ** (-ALIBI_MAX_BIAS * (gi * N_HEADS + h + 1) / (n_groups * N_HEADS))
                   for h in range(N_HEADS))

    rows_block = pl.BlockSpec((None, None, rows, ATT_WIDTH), lambda b, r, n: (b, r, n, 0))

    return pl.pallas_call(
        functools.partial(_att_group_kernel, dilation=d, slopes=slopes, nq=nq),
        grid=(batch, d, nblk // nq),
        in_specs=[rows_block, rows_block, rows_block,
                  _resident((1, ATT_WIDTH)), _resident((1, ATT_WIDTH))],
        out_specs=[rows_block,
                   pl.BlockSpec((None, None, rows, LANES), lambda b, r, n: (b, r, n, 0))],
        out_shape=[jax.ShapeDtypeStruct((batch, d, sub, ATT_WIDTH), BF16),
                   jax.ShapeDtypeStruct((batch, d, sub, LANES), F32)],
        scratch_shapes=[pltpu.VMEM((ATT_SPAN + rows, ATT_WIDTH), BF16),
                        pltpu.VMEM((N_HEADS // 2, LANES + ONES_ROWS, ATT_SPAN + rows), BF16),
                        pltpu.VMEM((N_HEADS // 2, 2 * ATT_SPAN, 2 * ATT_SPAN), F32)],
        compiler_params=_cparams(("parallel", "parallel", "arbitrary")),
        name=f"att_group{gi}",
    )(q, k, v, qw, kw)


def _ffn_tail(x, mixed16, wmix_ref, gain_ref, win_ref, wout_ref, y_ref, th):
    hidden = wout_ref.shape[0]
    x1 = x + _dot(mixed16, wmix_ref[...])
    xn = _rms_scale(x1, gain_ref[...]).astype(BF16)
    acc = x1
    for j in range(hidden // th):
        gate = _dot(xn, win_ref[:, j * th:(j + 1) * th])
        up = _dot(xn, win_ref[:, hidden + j * th:hidden + (j + 1) * th])
        act = (gate * _sigmoid(gate) * up).astype(BF16)
        acc = acc + _dot(act, wout_ref[j * th:(j + 1) * th, :])
    y_ref[...] = acc


def _out_ffn_kernel(x_ref, o_ref, wmix_ref, gain_ref, win_ref, wout_ref, y_ref, *, th):
    _ffn_tail(x_ref[...], o_ref[...], wmix_ref, gain_ref, win_ref, wout_ref, y_ref, th)


def _out_ffn(x, o, wmix, gain, win, wout, *, tm=1024, th=256):
    n, d = x.shape
    ko = o.shape[1]
    hidden = wout.shape[0]
    return pl.pallas_call(
        functools.partial(_out_ffn_kernel, th=th),
        grid=(n // tm,),
        in_specs=[pl.BlockSpec((tm, d), lambda i: (i, 0)),
                  pl.BlockSpec((tm, ko), lambda i: (i, 0)),
                  _resident((ko, d)),
                  _resident((1, d)),
                  _resident((d, 2 * hidden)),
                  _resident((hidden, d))],
        out_specs=pl.BlockSpec((tm, d), lambda i: (i, 0)),
        out_shape=jax.ShapeDtypeStruct((n, d), F32),
        compiler_params=_cparams(("parallel",)),
        name="out_ffn",
    )(x, o, wmix, gain, win, wout)


def _merge_out_ffn_kernel(x_ref, o0_ref, o1_ref, o2_ref, l0_ref, l1_ref, l2_ref,
                          wmix_ref, gain_ref, win_ref, wout_ref, y_ref, onat_ref, lnat_ref, *, th):
    tm = x_ref.shape[0]
    n_pairs = N_HEADS // 2
    outs, lses = [], []
    for g, ((window, d), o_ref, l_ref) in enumerate(zip(ATT_GROUPS, (o0_ref, o1_ref, o2_ref),
                                                        (l0_ref, l1_ref, l2_ref))):
        if d == 1:
            outs.append([o_ref[0, :, p * LANES:(p + 1) * LANES].astype(F32) for p in range(n_pairs)])
            lses.append(l_ref[0])
            continue
        rows = tm // d
        for r in range(d):
            for p in range(n_pairs):
                onat_ref[g - 1, p, pl.ds(r, rows, stride=d), :] = (
                    o_ref[r, :, p * LANES:(p + 1) * LANES].astype(F32))
            lnat_ref[g - 1, pl.ds(r, rows, stride=d), :] = l_ref[r]
        outs.append([onat_ref[g - 1, p] for p in range(n_pairs)])
        lses.append(lnat_ref[g - 1])
    mx = jnp.maximum(jnp.maximum(lses[0], lses[1]), lses[2])
    wts = [jnp.exp(l - mx) for l in lses]
    inv = 1.0 / (wts[0] + wts[1] + wts[2])
    wts = [w * inv for w in wts]
    left = lax.broadcasted_iota(jnp.int32, (tm, LANES), 1) < ATT_DH
    pairs = []
    for p in range(n_pairs):
        acc = None
        for w, o in zip(wts, outs):
            term = jnp.where(left, w[:, 2 * p:2 * p + 1], w[:, 2 * p + 1:2 * p + 2]) * o[p]
            acc = term if acc is None else acc + term
        pairs.append(acc.astype(BF16))
    mixed16 = jnp.concatenate(pairs, axis=1)
    _ffn_tail(x_ref[...], mixed16, wmix_ref, gain_ref, win_ref, wout_ref, y_ref, th)


def _merge_out_ffn(x, outs, lses, wmix, gain, win, wout, *, batch, seq, tm=512, th=256):
    n, d_model = x.shape
    nt = seq // tm
    hidden = wout.shape[0]
    o_specs = [pl.BlockSpec((None, d, tm // d, ATT_WIDTH), lambda b, i: (b, 0, i, 0))
               for window, d in ATT_GROUPS]
    l_specs = [pl.BlockSpec((None, d, tm // d, LANES), lambda b, i: (b, 0, i, 0))
               for window, d in ATT_GROUPS]
    return pl.pallas_call(
        functools.partial(_merge_out_ffn_kernel, th=th),
        grid=(batch, nt),
        in_specs=[pl.BlockSpec((tm, d_model), lambda b, i: (b * nt + i, 0))] + o_specs + l_specs
                 + [_resident((ATT_WIDTH, d_model)), _resident((1, d_model)),
                    _resident((d_model, 2 * hidden)), _resident((hidden, d_model))],
        out_specs=pl.BlockSpec((tm, d_model), lambda b, i: (b * nt + i, 0)),
        out_shape=jax.ShapeDtypeStruct((n, d_model), F32),
        scratch_shapes=[pltpu.VMEM((len(ATT_GROUPS) - 1, ATT_WIDTH // LANES, tm, LANES), F32),
                        pltpu.VMEM((len(ATT_GROUPS) - 1, tm, LANES), F32)],
        compiler_params=_cparams(("parallel", "parallel")),
        name="merge_out_ffn",
    )(x, *outs, *lses, wmix, gain, win, wout)


def _gdn_layer(xf, gain_mix, w_in, conv_w, a_log, dt_bias, norm_w, *, batch, seq):
    d_model = xf.shape[1]
    main = 4 * N_HEADS * GDN_DK
    wab = jnp.zeros((d_model, LANES), BF16).at[:, :2 * N_HEADS].set(w_in[:, main:].astype(BF16))
    gconst = jnp.zeros((8, LANES), F32)
    gconst = gconst.at[0, :N_HEADS].set(-jnp.exp(a_log.astype(F32)))
    gconst = gconst.at[1, :N_HEADS].set(dt_bias.astype(F32))
    proj, colpack, gct = _gdn_proj(xf, gain_mix, w_in[:, :main].astype(BF16), wab, gconst,
                                   conv_w.astype(F32), batch=batch, seq=seq)
    return _gdn_chunk(proj, colpack, gct, norm_w.reshape(1, GDN_DK).astype(F32),
                      batch=batch, seq=seq)


def _att_layer(xf, gain_mix, w_in, q_norm, k_norm, *, batch, seq):
    n_groups = len(ATT_GROUPS)
    w_in = w_in.astype(BF16)
    ws = [jnp.concatenate([w_in[:, (which * n_groups + gi) * ATT_WIDTH:
                                (which * n_groups + gi + 1) * ATT_WIDTH]
                           for which in range(3)], axis=1) for gi in range(n_groups)]
    projs = _dil_proj(xf, gain_mix, ws, batch=batch, seq=seq)
    qw = jnp.tile(q_norm.astype(F32), N_HEADS).reshape(1, ATT_WIDTH)
    kw = jnp.tile(k_norm.astype(F32), N_HEADS).reshape(1, ATT_WIDTH)
    res = [_att_group(*projs[3 * gi:3 * gi + 3], qw, kw, gi=gi, batch=batch, seq=seq)
           for gi in range(n_groups)]
    return [r[0] for r in res], [r[1] for r in res]


def kernel(x, norm_mix, norm_ffn, gdn_w_in, gdn_conv_w, gdn_a_log, gdn_dt_bias, gdn_norm_w,
           gdn_w_out, dil_w_in, dil_q_norm, dil_k_norm, dil_w_out, ffn_w_in, ffn_w_out):
    batch, seq, d_model = x.shape
    depth = norm_mix.shape[0]
    xf = x.reshape(batch * seq, d_model)
    for i in range(depth):
        j = i // 2
        gain_mix = norm_mix[i].reshape(1, d_model)
        ffn = (norm_ffn[i].reshape(1, d_model), ffn_w_in[i].astype(BF16), ffn_w_out[i].astype(BF16))
        if i % 2 == 0:
            mixed = _gdn_layer(xf, gain_mix, gdn_w_in[j], gdn_conv_w[j], gdn_a_log[j],
                               gdn_dt_bias[j], gdn_norm_w[j], batch=batch, seq=seq)
            xf = _out_ffn(xf, mixed, gdn_w_out[j].astype(BF16), *ffn)
        else:
            outs, lses = _att_layer(xf, gain_mix, dil_w_in[j], dil_q_norm[j], dil_k_norm[j],
                                    batch=batch, seq=seq)
            xf = _merge_out_ffn(xf, outs, lses, dil_w_out[j].astype(BF16), *ffn,
                                batch=batch, seq=seq)
    return xf.reshape(batch, seq, d_model)
```
